```python
import jax, jax.numpy as jnp
from jax import lax
import numpy as np


D_MODEL = 4096
BATCH = 4
SEQ = 2048
DEPTH = 4
DEC_BATCH = 128
DEC_SEQ = 8
PAST_LEN = 16384
PAGE_SIZE = 128

N_MIXERS = 2
N_LAYERS_A = (DEPTH + N_MIXERS - 1) // N_MIXERS
N_LAYERS_B = DEPTH // N_MIXERS
CHUNK = 128
D_A = D_MODEL
N_GROUPS_A = 16
GROUP_A = D_A // N_GROUPS_A
D_RNN = ((4 * D_MODEL // 3 + 255) // 256) * 256
N_BLOCKS_B = 16
BLOCK_B = D_RNN // N_BLOCKS_B
CONV_W = 4
LRU_C = 8.0
D_FF = ((8 * D_MODEL // 3 + 255) // 256) * 256
EPS = 1e-6

kernel_name = 'chunk_mlp_rglru_hybrid_step'


def rmsnorm(x, g):
    xf = x.astype(jnp.float32)
    y = xf * lax.rsqrt(jnp.mean(xf * xf, axis=-1, keepdims=True) + EPS)
    return (y * g.astype(jnp.float32)).astype(x.dtype)


def layernorm(x, g, b):
    xf = x.astype(jnp.float32)
    mu = jnp.mean(xf, axis=-1, keepdims=True)
    var = jnp.mean(jnp.square(xf - mu), axis=-1, keepdims=True)
    y = (xf - mu) * lax.rsqrt(var + EPS)
    return (y * g.astype(jnp.float32) + b.astype(jnp.float32)).astype(x.dtype)


def ada(c, w, b):
    mod = jax.nn.silu(c) @ w + b
    return jnp.split(mod, 6, axis=-1)


def modulate(h, shift, scale):
    return h * (1 + scale[:, None, :]) + shift[:, None, :]


def mixer_a(h, w_in, b_in, g_v, b_v, w_s, b_s, w_out):
    z = jax.nn.gelu(h @ w_in + b_in)
    u, v = jnp.split(z, 2, axis=-1)
    v = layernorm(v, g_v, b_v)
    B, T, _ = v.shape
    L = min(T, CHUNK)
    v5 = v.reshape(B, T // L, L, N_GROUPS_A, GROUP_A)
    causal = jnp.tril(jnp.ones((L, L), dtype=bool))
    w = jnp.where(causal[None], w_s[:, :L, :L], jnp.zeros((), w_s.dtype))
    sv = jnp.einsum('gts,bnsgd->bntgd', w, v5) + b_s[:, :L].T[:, :, None]
    y = u * sv.reshape(B, T, D_A)
    return y @ w_out, v


def blockdiag(x, w, b):
    B, T, _ = x.shape
    xr = x.reshape(B, T, N_BLOCKS_B, BLOCK_B)
    return jnp.einsum('btnd,nde->btne', xr, w).reshape(B, T, D_RNN) + b


def mixer_b(h, conv_buf, h0, w_in, b_in, w_conv, b_conv, w_ra, b_ra, w_ix, b_ix, lam, w_out):
    xg = h @ w_in + b_in
    g, xb = jnp.split(xg, 2, axis=-1)
    T = xb.shape[1]
    xpad = jnp.concatenate([conv_buf.astype(xb.dtype), xb], axis=1)
    xc = b_conv[None, None, :]
    for k in range(CONV_W):
        xc = xc + w_conv[k] * xpad[:, k:k + T]
    new_buf = xpad[:, T:]
    r = jax.nn.sigmoid(blockdiag(xc, w_ra, b_ra).astype(jnp.float32))
    i = jax.nn.sigmoid(blockdiag(xc, w_ix, b_ix).astype(jnp.float32))
    log_a = -LRU_C * r * jax.nn.softplus(-lam.astype(jnp.float32))
    a = jnp.exp(log_a)
    bx = jnp.sqrt(-jnp.expm1(2.0 * log_a)) * (i * xc.astype(jnp.float32))

    def step(hc, inp):
        a_t, b_t = inp
        hn = a_t * hc + b_t
        return hn, hn

    h_last, hs = lax.scan(step, h0.astype(jnp.float32),
                          (jnp.swapaxes(a, 0, 1), jnp.swapaxes(bx, 0, 1)))
    hs = jnp.swapaxes(hs, 0, 1).astype(g.dtype)
    y = jax.nn.gelu(g) * hs
    return y @ w_out, new_buf, h_last.astype(h0.dtype)


def swiglu(h, w1, w3, w2):
    return (jax.nn.silu(h @ w1) * (h @ w3)) @ w2


def setup_inputs(seed: int = 0) -> dict:
    key = jax.random.key(seed)
    ks = jax.random.split(key, 40)
    f32 = jnp.float32
    nrm = lambda k, s, sc: jax.random.normal(k, s, f32) * sc
    D = D_MODEL
    a_init = jax.random.uniform(ks[30], (N_LAYERS_B, D_RNN), f32, 0.9, 0.999)
    return {
        'x_prompt': nrm(ks[0], (BATCH, SEQ, D), 1.0),
        'x_sample': nrm(ks[1], (DEC_BATCH, DEC_SEQ, D), 1.0),
        'state_conv': nrm(ks[2], (N_LAYERS_B, DEC_BATCH, CONV_W - 1, D_RNN), 0.5),
        'state_h': nrm(ks[3], (N_LAYERS_B, DEC_BATCH, D_RNN), 0.5),
        'c_prompt': nrm(ks[4], (BATCH, D), 1.0),
        'c_sample': nrm(ks[5], (DEC_BATCH, D), 1.0),
        'w_ada': nrm(ks[6], (DEPTH, D, 6 * D), 0.5 * D ** -0.5),
        'b_ada': nrm(ks[7], (DEPTH, 6 * D), 0.02),
        'g_norm1': 1.0 + nrm(ks[8], (DEPTH, D), 0.05),
        'g_norm2': 1.0 + nrm(ks[9], (DEPTH, D), 0.05),
        'g_final': 1.0 + nrm(ks[10], (D,), 0.05),
        'w_in_a': nrm(ks[11], (N_LAYERS_A, D, 2 * D_A), D ** -0.5),
        'b_in_a': nrm(ks[12], (N_LAYERS_A, 2 * D_A), 0.02),
        'g_v_a': 1.0 + nrm(ks[13], (N_LAYERS_A, D_A), 0.05),
        'b_v_a': nrm(ks[14], (N_LAYERS_A, D_A), 0.05),
        'w_s_a': nrm(ks[15], (N_LAYERS_A, N_GROUPS_A, CHUNK, CHUNK), CHUNK ** -0.5),
        'b_s_a': 1.0 + nrm(ks[16], (N_LAYERS_A, N_GROUPS_A, CHUNK), 0.1),
        'w_out_a': nrm(ks[17], (N_LAYERS_A, D_A, D), D_A ** -0.5),
        'w_in_b': nrm(ks[18], (N_LAYERS_B, D, 2 * D_RNN), D ** -0.5),
        'b_in_b': nrm(ks[19], (N_LAYERS_B, 2 * D_RNN), 0.02),
        'w_conv_b': nrm(ks[20], (N_LAYERS_B, CONV_W, D_RNN), CONV_W ** -0.5),
        'b_conv_b': nrm(ks[21], (N_LAYERS_B, D_RNN), 0.02),
        'w_ra_b': nrm(ks[22], (N_LAYERS_B, N_BLOCKS_B, BLOCK_B, BLOCK_B), BLOCK_B ** -0.5),
        'b_ra_b': nrm(ks[23], (N_LAYERS_B, D_RNN), 0.02),
        'w_ix_b': nrm(ks[24], (N_LAYERS_B, N_BLOCKS_B, BLOCK_B, BLOCK_B), BLOCK_B ** -0.5),
        'b_ix_b': nrm(ks[25], (N_LAYERS_B, D_RNN), 0.02),
        'lam_b': jnp.log(a_init) - jnp.log1p(-a_init),
        'w_out_b': nrm(ks[26], (N_LAYERS_B, D_RNN, D), D_RNN ** -0.5),
        'w_ff1': nrm(ks[27], (DEPTH, D, D_FF), D ** -0.5),
        'w_ff3': nrm(ks[28], (DEPTH, D, D_FF), D ** -0.5),
        'w_ff2': nrm(ks[29], (DEPTH, D_FF, D), D_FF ** -0.5),
    }


def reference(x_prompt, x_sample, state_conv, state_h, c_prompt, c_sample,
              w_ada, b_ada, g_norm1, g_norm2, g_final,
              w_in_a, b_in_a, g_v_a, b_v_a, w_s_a, b_s_a, w_out_a,
              w_in_b, b_in_b, w_conv_b, b_conv_b, w_ra_b, b_ra_b, w_ix_b, b_ix_b, lam_b, w_out_b,
              w_ff1, w_ff3, w_ff2):
    xp, xs = x_prompt, x_sample
    v_new, conv_p, h_p, conv_s, h_s = [], [], [], [], []
    for i in range(DEPTH):
        j = i // N_MIXERS
        sh1p, sc1p, gt1p, sh2p, sc2p, gt2p = ada(c_prompt, w_ada[i], b_ada[i])
        sh1s, sc1s, gt1s, sh2s, sc2s, gt2s = ada(c_sample, w_ada[i], b_ada[i])
        hp = modulate(rmsnorm(xp, g_norm1[i]), sh1p, sc1p)
        hs_ = modulate(rmsnorm(xs, g_norm1[i]), sh1s, sc1s)
        if i % N_MIXERS == 0:
            wa = (w_in_a[j], b_in_a[j], g_v_a[j], b_v_a[j], w_s_a[j], b_s_a[j], w_out_a[j])
            op, _ = mixer_a(hp, *wa)
            os_, v_s = mixer_a(hs_, *wa)
            v_new.append(v_s)
        else:
            wb = (w_in_b[j], b_in_b[j], w_conv_b[j], b_conv_b[j], w_ra_b[j], b_ra_b[j],
                  w_ix_b[j], b_ix_b[j], lam_b[j], w_out_b[j])
            buf0 = jnp.zeros((xp.shape[0], CONV_W - 1, D_RNN), xp.dtype)
            h00 = jnp.zeros((xp.shape[0], D_RNN), xp.dtype)
            op, cbp, hlp = mixer_b(hp, buf0, h00, *wb)
            os_, cbs, hls = mixer_b(hs_, state_conv[j], state_h[j], *wb)
            conv_p.append(cbp)
            h_p.append(hlp)
            conv_s.append(cbs)
            h_s.append(hls)
        xp = xp + gt1p[:, None, :] * op
        xs = xs + gt1s[:, None, :] * os_
        hp = modulate(rmsnorm(xp, g_norm2[i]), sh2p, sc2p)
        hs_ = modulate(rmsnorm(xs, g_norm2[i]), sh2s, sc2s)
        xp = xp + gt2p[:, None, :] * swiglu(hp, w_ff1[i], w_ff3[i], w_ff2[i])
        xs = xs + gt2s[:, None, :] * swiglu(hs_, w_ff1[i], w_ff3[i], w_ff2[i])
    y_prompt = rmsnorm(xp, g_final)
    y_sample = rmsnorm(xs, g_final)
    new_v_sample = jnp.stack(v_new)
    new_conv_prompt = jnp.stack(conv_p)
    new_h_prompt = jnp.stack(h_p)
    new_conv_sample = jnp.stack(conv_s)
    new_h_sample = jnp.stack(h_s)
    return (y_prompt, y_sample, new_v_sample, new_conv_prompt, new_h_prompt, new_conv_sample, new_h_sample)
```

```python
import functools
import math

import jax
import jax.numpy as jnp
from jax import lax
from jax.experimental import pallas as pl
from jax.experimental.pallas import tpu as pltpu

EPS = 1e-6
LRU_C = 8.0
CHUNK = 128
N_GROUPS_A = 16
N_BLOCKS_B = 16
CONV_W = 4

LANES = 128
SUBLANES = 8
PAT = 128
VMEM_LIMIT_BYTES = 56 * 1024 * 1024

F32 = jnp.float32
BF16 = jnp.bfloat16


def _params(*sem):
    return pltpu.CompilerParams(dimension_semantics=sem, vmem_limit_bytes=VMEM_LIMIT_BYTES)


def _sigmoid(x):
    return 0.5 * (jnp.tanh(0.5 * x) + 1.0)


def _ada_kernel(c_ref, w_ref, b_ref, o_ref, act_ref):
    @pl.when((pl.program_id(0) == 0) & (pl.program_id(1) == 0))
    def _():
        c = c_ref[...]
        act_ref[...] = (c * _sigmoid(c)).astype(BF16)

    w = w_ref[...].astype(BF16)
    o_ref[...] = jnp.dot(act_ref[...], w, preferred_element_type=F32) + b_ref[...]


def _ada(c_pat, w_ada, b_ada, bn=512):
    depth, d, n6 = w_ada.shape
    rows = c_pat.shape[0]
    return pl.pallas_call(
        _ada_kernel,
        grid=(depth, n6 // bn),
        in_specs=[
            pl.BlockSpec((rows, d), lambda i, n: (0, 0)),
            pl.BlockSpec((None, d, bn), lambda i, n: (i, 0, n)),
            pl.BlockSpec((None, 1, bn), lambda i, n: (i, 0, n)),
        ],
        out_specs=pl.BlockSpec((None, rows, bn), lambda i, n: (i, 0, n)),
        out_shape=jax.ShapeDtypeStruct((depth, rows, n6), F32),
        scratch_shapes=[pltpu.VMEM((rows, d), BF16)],
        compiler_params=_params("arbitrary", "arbitrary"),
        name="ada",
    )(c_pat, w_ada, b_ada.reshape(depth, 1, n6))


def _rms_mod_kernel(x_ref, g_ref, sh_ref, sc_ref, o_ref):
    g = g_ref[...]
    scale = 1.0 + sc_ref[...]
    shift = sh_ref[...]
    for k in range(x_ref.shape[0] // PAT):
        rows = slice(k * PAT, (k + 1) * PAT)
        x = x_ref[rows, :]
        y = x * lax.rsqrt(jnp.mean(x * x, axis=-1, keepdims=True) + EPS) * g
        o_ref[rows, :] = (y * scale + shift).astype(o_ref.dtype)


def _rms_mod(x, g, mod, layer, piece_shift, piece_scale, mp, bm=512):
    m, d = x.shape
    n_prompt_tiles = mp // bm

    def pat(piece):
        return lambda i: (layer, jnp.where(i >= n_prompt_tiles, 1, 0), piece)

    return pl.pallas_call(
        _rms_mod_kernel,
        grid=(m // bm,),
        in_specs=[
            pl.BlockSpec((bm, d), lambda i: (i, 0)),
            pl.BlockSpec((1, d), lambda i: (0, 0)),
            pl.BlockSpec((None, PAT, d), pat(piece_shift)),
            pl.BlockSpec((None, PAT, d), pat(piece_scale)),
        ],
        out_specs=pl.BlockSpec((bm, d), lambda i: (i, 0)),
        out_shape=jax.ShapeDtypeStruct((m, d), BF16),
        compiler_params=_params("arbitrary"),
        name="rms_mod",
    )(x, g.reshape(1, d), mod, mod)


def _rms_kernel(x_ref, g_ref, o_ref):
    x = x_ref[...]
    o_ref[...] = x * lax.rsqrt(jnp.mean(x * x, axis=-1, keepdims=True) + EPS) * g_ref[...]


def _rms(x, g, bm=256):
    m, d = x.shape
    return pl.pallas_call(
        _rms_kernel,
        grid=(m // bm,),
        in_specs=[pl.BlockSpec((bm, d), lambda i: (i, 0)), pl.BlockSpec((1, d), lambda i: (0, 0))],
        out_specs=pl.BlockSpec((bm, d), lambda i: (i, 0)),
        out_shape=jax.ShapeDtypeStruct((m, d), F32),
        compiler_params=_params("arbitrary"),
        name="rms_final",
    )(x, g.reshape(1, d))


def _layernorm_kernel(x_ref, g_ref, b_ref, o_ref):
    x = x_ref[...]
    mu = jnp.mean(x, axis=-1, keepdims=True)
    xc = x - mu
    var = jnp.mean(xc * xc, axis=-1, keepdims=True)
    o_ref[...] = (xc * lax.rsqrt(var + EPS) * g_ref[...] + b_ref[...]).astype(o_ref.dtype)


def _layernorm(x, g, b, out_dtype, row0, nrows, bm=256):
    d = x.shape[1]
    off = row0 // bm
    return pl.pallas_call(
        _layernorm_kernel,
        grid=(nrows // bm,),
        in_specs=[
            pl.BlockSpec((bm, d), lambda i: (i + off, 0)),
            pl.BlockSpec((1, d), lambda i: (0, 0)),
            pl.BlockSpec((1, d), lambda i: (0, 0)),
        ],
        out_specs=pl.BlockSpec((bm, d), lambda i: (i, 0)),
        out_shape=jax.ShapeDtypeStruct((nrows, d), out_dtype),
        compiler_params=_params("arbitrary"),
        name="layernorm",
    )(x, g.reshape(1, d), b.reshape(1, d))


def _mm_bias_act_kernel(x_ref, w_ref, b_ref, o_ref, *, act):
    acc = jnp.dot(x_ref[...], w_ref[...], preferred_element_type=F32) + b_ref[...]
    if act == "gelu":
        acc = jax.nn.gelu(acc)
    o_ref[...] = acc.astype(o_ref.dtype)


def _mm_bias_act(x, w, b, layer, col0, ncols, act, out_dtype, bm=1024, bn=512):
    m, k = x.shape
    noff = col0 // bn
    b3 = b.reshape(b.shape[0], 1, b.shape[1])
    return pl.pallas_call(
        functools.partial(_mm_bias_act_kernel, act=act),
        grid=(m // bm, ncols // bn),
        in_specs=[
            pl.BlockSpec((bm, k), lambda i, n: (i, 0)),
            pl.BlockSpec((None, k, bn), lambda i, n: (layer, 0, n + noff)),
            pl.BlockSpec((None, 1, bn), lambda i, n: (layer, 0, n + noff)),
        ],
        out_specs=pl.BlockSpec((bm, bn), lambda i, n: (i, n)),
        out_shape=jax.ShapeDtypeStruct((m, ncols), out_dtype),
        compiler_params=_params("arbitrary", "arbitrary"),
        name="mm_bias_act",
    )(x, w, b3)


def _mm_swiglu_kernel(x_ref, w1_ref, w3_ref, o_ref):
    x = x_ref[...]
    a = jnp.dot(x, w1_ref[...], preferred_element_type=F32)
    b = jnp.dot(x, w3_ref[...], preferred_element_type=F32)
    o_ref[...] = (a * _sigmoid(a) * b).astype(o_ref.dtype)


def _mm_swiglu(x, w1, w3, layer, bm=1024, bn=256):
    m, k = x.shape
    n = w1.shape[2]
    wspec = pl.BlockSpec((None, k, bn), lambda i, j: (layer, 0, j))
    return pl.pallas_call(
        _mm_swiglu_kernel,
        grid=(m // bm, n // bn),
        in_specs=[pl.BlockSpec((bm, k), lambda i, j: (i, 0)), wspec, wspec],
        out_specs=pl.BlockSpec((bm, bn), lambda i, j: (i, j)),
        out_shape=jax.ShapeDtypeStruct((m, n), BF16),
        compiler_params=_params("arbitrary", "arbitrary"),
        name="mm_swiglu",
    )(x, w1, w3)


def _mm_resid_kernel(x_ref, w_ref, res_ref, gate_ref, o_ref):
    acc = jnp.dot(x_ref[...], w_ref[...], preferred_element_type=F32)
    gate = gate_ref[...]
    for k in range(acc.shape[0] // PAT):
        rows = slice(k * PAT, (k + 1) * PAT)
        o_ref[rows, :] = res_ref[rows, :] + gate * acc[rows, :]


def _mm_resid(x, w, layer_w, res, mod, layer, piece_gate, mp, bm, bn):
    m, k = x.shape
    n = w.shape[2]
    n_prompt_tiles = mp // bm
    gate_col0 = piece_gate * (n // bn)
    return pl.pallas_call(
        _mm_resid_kernel,
        grid=(m // bm, n // bn),
        in_specs=[
            pl.BlockSpec((bm, k), lambda i, j: (i, 0)),
            pl.BlockSpec((None, k, bn), lambda i, j: (layer_w, 0, j)),
            pl.BlockSpec((bm, bn), lambda i, j: (i, j)),
            pl.BlockSpec((None, PAT, bn), lambda i, j: (layer, jnp.where(i >= n_prompt_tiles, 1, 0), gate_col0 + j)),
        ],
        out_specs=pl.BlockSpec((bm, bn), lambda i, j: (i, j)),
        out_shape=jax.ShapeDtypeStruct((m, n), F32),
        input_output_aliases={2: 0},
        compiler_params=_params("arbitrary", "arbitrary"),
        name="mm_resid",
    )(x, w, res, mod)


def _gate_prompt_kernel(u_ref, v_ref, mix_ref, bias_ref, y_ref, *, group):
    for g in range(mix_ref.shape[0]):
        cols = slice(g * group, (g + 1) * group)
        sv = jnp.dot(mix_ref[g], v_ref[:, cols], preferred_element_type=F32) + bias_ref[g]
        y_ref[:, cols] = (u_ref[:, cols].astype(F32) * sv).astype(y_ref.dtype)


def _gate_prompt(u, vn, mix, bias, mp, group, col_tiles=2):
    m, d = u.shape
    n_groups, r, _ = mix.shape
    gpt = n_groups // col_tiles
    bw = gpt * group
    return pl.pallas_call(
        functools.partial(_gate_prompt_kernel, group=group),
        grid=(col_tiles, mp // r),
        in_specs=[
            pl.BlockSpec((r, bw), lambda h, c: (c, h)),
            pl.BlockSpec((r, bw), lambda h, c: (c, h)),
            pl.BlockSpec((gpt, r, r), lambda h, c: (h, 0, 0)),
            pl.BlockSpec((gpt, r, 1), lambda h, c: (h, 0, 0)),
        ],
        out_specs=pl.BlockSpec((r, bw), lambda h, c: (c, h)),
        out_shape=jax.ShapeDtypeStruct((m, d), BF16),
        compiler_params=_params("arbitrary", "arbitrary"),
        name="gate_prompt",
    )(u, vn, mix, bias)


def _gate_sample_kernel(w_ref, b_ref, u_ref, v_ref, y_any_ref, y_ref, *, steps, p):
    del y_any_ref
    g = pl.program_id(0)
    for t in range(steps):
        acc = w_ref[g, t * steps] * v_ref[0:p, :]
        for s in range(1, t + 1):
            acc = acc + w_ref[g, t * steps + s] * v_ref[s * p:(s + 1) * p, :]
        acc = acc + b_ref[g, t]
        rows = slice(t * p, (t + 1) * p)
        y_ref[rows, :] = (u_ref[rows, :].astype(F32) * acc).astype(y_ref.dtype)


def _gate_sample(w_small, b_small, u, vn_s, y, mp, steps, p, group):
    ms = steps * p
    n_groups = w_small.shape[0]
    row_blk = mp // ms
    return pl.pallas_call(
        functools.partial(_gate_sample_kernel, steps=steps, p=p),
        grid=(n_groups,),
        in_specs=[
            pl.BlockSpec(memory_space=pltpu.SMEM),
            pl.BlockSpec(memory_space=pltpu.SMEM),
            pl.BlockSpec((ms, group), lambda g: (row_blk, g)),
            pl.BlockSpec((ms, group), lambda g: (0, g)),
            pl.BlockSpec(memory_space=pl.ANY),
        ],
        out_specs=pl.BlockSpec((ms, group), lambda g: (row_blk, g)),
        out_shape=jax.ShapeDtypeStruct(y.shape, y.dtype),
        input_output_aliases={4: 0},
        compiler_params=_params("arbitrary"),
        name="gate_sample",
    )(w_small, b_small, u, vn_s, y)


def _window_starts(block, blocks_per_group, group_w, win):
    return [min((bl * block) // LANES * LANES, group_w - win) for bl in range(blocks_per_group)]


def _rglru_kernel(*refs, p, halo, starts, win, aliased):
    if aliased:
        refs = refs[1:]
    (xb_ref, gg_ref, hist_ref, h0_ref, wc_ref, bc_ref, wra_ref, bra_ref, wix_ref, bix_ref, lam_ref,
     y_ref, conv_ref, hlast_ref, s_ref, xc_ref, xcb_ref, a_ref, bx_ref, hc_ref) = refs
    t = pl.program_id(1)
    bt, gw = xb_ref.shape

    @pl.when(t == 0)
    def _():
        s_ref[0:halo, :] = hist_ref[...]
        hc_ref[...] = h0_ref[...]

    s_ref[halo:halo + bt, :] = xb_ref[...]
    xc = bc_ref[...]
    for k in range(CONV_W):
        lo = halo - (CONV_W - 1 - k) * p
        xc = xc + wc_ref[k:k + 1, :] * s_ref[lo:lo + bt, :]
    xc_ref[...] = xc
    xcb_ref[...] = xc.astype(BF16)
    s_ref[0:halo, :] = s_ref[bt:bt + halo, :]

    a_ref[...] = jnp.broadcast_to(bra_ref[...], (bt, gw))
    bx_ref[...] = jnp.broadcast_to(bix_ref[...], (bt, gw))
    for bl, lo in enumerate(starts):
        xw = xcb_ref[:, lo:lo + win]
        a_ref[:, lo:lo + win] += jnp.dot(xw, wra_ref[bl], preferred_element_type=F32)
        bx_ref[:, lo:lo + win] += jnp.dot(xw, wix_ref[bl], preferred_element_type=F32)

    lam = lam_ref[...]
    c8 = -LRU_C * (jnp.maximum(-lam, 0.0) + jnp.log1p(jnp.exp(-jnp.abs(lam))))

    rc = 32 if bt % 32 == 0 else bt

    def gates(i, carry):
        rows = pl.ds(pl.multiple_of(i * rc, rc), rc)
        r = _sigmoid(a_ref[rows, :])
        ig = _sigmoid(bx_ref[rows, :])
        log_a = c8 * r
        a = jnp.exp(log_a)
        mult = jnp.sqrt(-jnp.tanh(log_a) * (1.0 + a * a))
        a_ref[rows, :] = a
        bx_ref[rows, :] = mult * (ig * xc_ref[rows, :])
        return carry

    lax.fori_loop(0, bt // rc, gates, 0)

    if p % SUBLANES == 0:
        h = hc_ref[...]
        for step in range(bt // p):
            rows = slice(step * p, (step + 1) * p)
            h = a_ref[rows, :] * h + bx_ref[rows, :]
            y_ref[rows, :] = (gg_ref[rows, :].astype(F32) * h).astype(y_ref.dtype)
        hc_ref[...] = h
    else:
        strip = 2 * SUBLANES
        cw_max = 3 * LANES
        col = 0
        while col < gw:
            cw = min(cw_max, gw - col)
            cols = slice(col, col + cw)
            upper = lax.broadcasted_iota(jnp.int32, (SUBLANES, cw), 0) >= p

            def one_group(a8, b8, hc):
                a_prev = pltpu.roll(a8, p, 0)
                b_prev = pltpu.roll(b8, p, 0)
                a2 = jnp.where(upper, a8 * a_prev, a8)
                b2 = jnp.where(upper, a8 * b_prev + b8, b8)
                h = a2 * hc + b2
                return h, jnp.where(upper, h, pltpu.roll(h, p, 0))

            def scan(i, hc, cols=cols, one_group=one_group):
                rows = pl.ds(pl.multiple_of(i * strip, strip), strip)
                a16 = a_ref[rows, cols]
                b16 = bx_ref[rows, cols]
                h_lo, hc = one_group(a16[:SUBLANES], b16[:SUBLANES], hc)
                h_hi, hc = one_group(a16[SUBLANES:], b16[SUBLANES:], hc)
                h16 = jnp.concatenate([h_lo, h_hi], axis=0)
                y_ref[rows, cols] = (gg_ref[rows, cols].astype(F32) * h16).astype(y_ref.dtype)
                return hc

            hc_ref[:, cols] = lax.fori_loop(0, bt // strip, scan, hc_ref[:, cols])
            col += cw

    @pl.when(t == pl.num_programs(1) - 1)
    def _():
        conv_ref[...] = s_ref[0:halo, :]
        hlast_ref[...] = hc_ref[...]


def _rglru(xb, gg, hist, h0, wconv, bconv, wra, bra, wix, bix, lam, y_prev, *, row0, nrows, p, bt,
           block, blocks_per_group, win):
    m, c = xb.shape
    gw = block * blocks_per_group
    n_groups = c // gw
    halo = hist.shape[0]
    hrows = h0.shape[0]
    starts = _window_starts(block, blocks_per_group, gw, win)
    rb = row0 // bt
    aliased = y_prev is not None

    row_blk = lambda g, t: (rb + t, g)
    vec = pl.BlockSpec((1, gw), lambda g, t: (0, g))
    wspec = pl.BlockSpec((blocks_per_group, win, win), lambda g, t: (g, 0, 0))
    in_specs = [
        pl.BlockSpec((bt, gw), row_blk),
        pl.BlockSpec((bt, gw), row_blk),
        pl.BlockSpec((halo, gw), lambda g, t: (0, g)),
        pl.BlockSpec((hrows, gw), lambda g, t: (0, g)),
        pl.BlockSpec((CONV_W, gw), lambda g, t: (0, g)),
        vec, wspec, vec, wspec, vec, vec,
    ]
    args = [xb, gg, hist, h0, wconv, bconv.reshape(1, c), wra, bra.reshape(1, c), wix, bix.reshape(1, c),
            lam.reshape(1, c)]
    aliases = {}
    if aliased:
        in_specs = [pl.BlockSpec(memory_space=pl.ANY)] + in_specs
        args = [y_prev] + args
        aliases = {0: 0}
    return pl.pallas_call(
        functools.partial(_rglru_kernel, p=p, halo=halo, starts=starts, win=win, aliased=aliased),
        grid=(n_groups, nrows // bt),
        in_specs=in_specs,
        out_specs=[
            pl.BlockSpec((bt, gw), row_blk),
            pl.BlockSpec((halo, gw), lambda g, t: (0, g)),
            pl.BlockSpec((hrows, gw), lambda g, t: (0, g)),
        ],
        out_shape=[
            jax.ShapeDtypeStruct((m, c), BF16),
            jax.ShapeDtypeStruct((halo, c), F32),
            jax.ShapeDtypeStruct((hrows, c), F32),
        ],
        scratch_shapes=[
            pltpu.VMEM((halo + bt, gw), F32),
            pltpu.VMEM((bt, gw), F32),
            pltpu.VMEM((bt, gw), BF16),
            pltpu.VMEM((bt, gw), F32),
            pltpu.VMEM((bt, gw), F32),
            pltpu.VMEM((hrows, gw), F32),
        ],
        input_output_aliases=aliases,
        compiler_params=_params("arbitrary", "arbitrary"),
        name="rglru",
    )(*args)


def _pad_block_weights(w, block, blocks_per_group, win):
    gw = block * blocks_per_group
    starts = _window_starts(block, blocks_per_group, gw, win)
    out = jnp.zeros((w.shape[0], win, win), BF16)
    for bl, lo in enumerate(starts):
        off = bl * block - lo
        out = out.at[bl::blocks_per_group, off:off + block, off:off + block].set(w[bl::blocks_per_group].astype(BF16))
    return out


def kernel(x_prompt, x_sample, state_conv, state_h, c_prompt, c_sample, w_ada, b_ada, g_norm1, g_norm2, g_final,
           w_in_a, b_in_a, g_v_a, b_v_a, w_s_a, b_s_a, w_out_a, w_in_b, b_in_b, w_conv_b, b_conv_b, w_ra_b, b_ra_b,
           w_ix_b, b_ix_b, lam_b, w_out_b, w_ff1, w_ff3, w_ff2):
    nb, t_p, d = x_prompt.shape
    ns, t_s, _ = x_sample.shape
    depth = w_ada.shape[0]
    d_a = w_out_a.shape[1]
    d_rnn = w_out_b.shape[1]
    group = d_a // N_GROUPS_A
    block = d_rnn // N_BLOCKS_B
    mp, ms = nb * t_p, ns * t_s
    assert ns == PAT and PAT % nb == 0 and t_p % CHUNK == 0 and t_s <= CHUNK and nb == 4

    x = jnp.concatenate([x_prompt.transpose(1, 0, 2).reshape(mp, d), x_sample.transpose(1, 0, 2).reshape(ms, d)])
    c_pat = jnp.concatenate([jnp.tile(c_prompt, (PAT // nb, 1)), c_sample])
    mod = _ada(c_pat, w_ada, b_ada)

    bf = lambda w: w.astype(BF16)
    w_in_a, w_out_a, w_in_b, w_out_b = bf(w_in_a), bf(w_out_a), bf(w_in_b), bf(w_out_b)
    w_ff1, w_ff3, w_ff2 = bf(w_ff1), bf(w_ff3), bf(w_ff2)

    blocks_per_group = LANES // math.gcd(block, LANES)
    win = (-(-block // LANES) + 1) * LANES
    halo_p = -(-(CONV_W - 1) * nb // SUBLANES) * SUBLANES

    v_new, conv_p, h_p, conv_s, h_s = [], [], [], [], []
    for i in range(depth):
        j = i // 2
        h = _rms_mod(x, g_norm1[i], mod, i, 0, 1, mp)
        if i % 2 == 0:
            u = _mm_bias_act(h, w_in_a, b_in_a, j, 0, d_a, "gelu", BF16)
            v = _mm_bias_act(h, w_in_a, b_in_a, j, d_a, d_a, "gelu", F32)
            vn = _layernorm(v, g_v_a[j], b_v_a[j], BF16, 0, mp)
            vn_s = _layernorm(v, g_v_a[j], b_v_a[j], F32, mp, ms)
            w_tril = jnp.where(jnp.tril(jnp.ones((CHUNK, CHUNK), bool)), w_s_a[j], 0.0)
            mix = jnp.einsum("gtu,sv->gtsuv", w_tril, jnp.eye(nb, dtype=F32)).reshape(N_GROUPS_A, CHUNK * nb, CHUNK * nb)
            bias = jnp.repeat(b_s_a[j], nb, axis=1)[:, :, None]
            y = _gate_prompt(u, vn, bf(mix), bias, mp, group)
            w_small = jnp.where(jnp.tril(jnp.ones((t_s, t_s), bool)), w_s_a[j][:, :t_s, :t_s], 0.0)
            y = _gate_sample(w_small.reshape(N_GROUPS_A, t_s * t_s), b_s_a[j][:, :t_s], u, vn_s, y, mp, t_s, ns, group)
            v_new.append(vn_s.reshape(t_s, ns, d_a).transpose(1, 0, 2))
            x = _mm_resid(y, w_out_a, j, x, mod, i, 2, mp, 1024, 512)
        else:
            gg = _mm_bias_act(h, w_in_b, b_in_b, j, 0, d_rnn, "gelu", BF16)
            xb = _mm_bias_act(h, w_in_b, b_in_b, j, d_rnn, d_rnn, None, F32)
            wra = _pad_block_weights(w_ra_b[j], block, blocks_per_group, win)
            wix = _pad_block_weights(w_ix_b[j], block, blocks_per_group, win)
            common = (w_conv_b[j], b_conv_b[j], wra, b_ra_b[j], wix, b_ix_b[j], lam_b[j])
            geom = dict(block=block, blocks_per_group=blocks_per_group, win=win)
            y, cp, hp = _rglru(xb, gg, jnp.zeros((halo_p, d_rnn), F32), jnp.zeros((SUBLANES, d_rnn), F32), *common,
                               None, row0=0, nrows=mp, p=nb, bt=512, **geom)
            hist_s = state_conv[j].transpose(1, 0, 2).reshape((CONV_W - 1) * ns, d_rnn)
            y, cs, hs = _rglru(xb, gg, hist_s, state_h[j], *common, y, row0=mp, nrows=ms, p=ns, bt=ns, **geom)
            conv_p.append(cp[halo_p - (CONV_W - 1) * nb:].reshape(CONV_W - 1, nb, d_rnn).transpose(1, 0, 2))
            h_p.append(hp[SUBLANES - nb:])
            conv_s.append(cs.reshape(CONV_W - 1, ns, d_rnn).transpose(1, 0, 2))
            h_s.append(hs)
            x = _mm_resid(y, w_out_b, j, x, mod, i, 2, mp, 1024, 512)
        h = _rms_mod(x, g_norm2[i], mod, i, 3, 4, mp)
        f = _mm_swiglu(h, w_ff1, w_ff3, i)
        x = _mm_resid(f, w_ff2, i, x, mod, i, 5, mp, 512, 256)

    yf = _rms(x, g_final)
    y_prompt = yf[:mp].reshape(t_p, nb, d).transpose(1, 0, 2)
    y_sample = yf[mp:].reshape(t_s, ns, d).transpose(1, 0, 2)
    return (y_prompt, y_sample, jnp.stack(v_new), jnp.stack(conv_p), jnp.stack(h_p), jnp.stack(conv_s),
            jnp.stack(h_s))
```

```python
import functools
import math

import jax
import jax.numpy as jnp
from jax import lax
from jax.experimental import pallas as pl
from jax.experimental.pallas import tpu as pltpu

EPS = 1e-6
LRU_C = 8.0
CHUNK = 128
N_GROUPS_A = 16
N_BLOCKS_B = 16
CONV_W = 4

LANES = 128
SUBLANES = 8
PAT = 128
VMEM_LIMIT_BYTES = 56 * 1024 * 1024

F32 = jnp.float32
BF16 = jnp.bfloat16


def _params(*sem):
    return pltpu.CompilerParams(dimension_semantics=sem, vmem_limit_bytes=VMEM_LIMIT_BYTES)


def _sigmoid(x):
    return 0.5 * (jnp.tanh(0.5 * x) + 1.0)


def _ada_kernel(c_ref, w_ref, b_ref, o_ref, act_ref):
    @pl.when((pl.program_id(0) == 0) & (pl.program_id(1) == 0))
    def _():
        c = c_ref[...]
        act_ref[...] = (c * _sigmoid(c)).astype(BF16)

    w = w_ref[...].astype(BF16)
    o_ref[...] = jnp.dot(act_ref[...], w, preferred_element_type=F32) + b_ref[...]


def _ada(c_pat, w_ada, b_ada, bn=512):
    depth, d, n6 = w_ada.shape
    rows = c_pat.shape[0]
    return pl.pallas_call(
        _ada_kernel,
        grid=(depth, n6 // bn),
        in_specs=[
            pl.BlockSpec((rows, d), lambda i, n: (0, 0)),
            pl.BlockSpec((None, d, bn), lambda i, n: (i, 0, n)),
            pl.BlockSpec((None, 1, bn), lambda i, n: (i, 0, n)),
        ],
        out_specs=pl.BlockSpec((None, rows, bn), lambda i, n: (i, 0, n)),
        out_shape=jax.ShapeDtypeStruct((depth, rows, n6), F32),
        scratch_shapes=[pltpu.VMEM((rows, d), BF16)],
        compiler_params=_params("arbitrary", "arbitrary"),
        name="ada",
    )(c_pat, w_ada, b_ada.reshape(depth, 1, n6))


def _rms_mod_kernel(x_ref, g_ref, sh_ref, sc_ref, o_ref):
    g = g_ref[...]
    scale = 1.0 + sc_ref[...]
    shift = sh_ref[...]
    for k in range(x_ref.shape[0] // PAT):
        rows = slice(k * PAT, (k + 1) * PAT)
        x = x_ref[rows, :]
        y = x * lax.rsqrt(jnp.mean(x * x, axis=-1, keepdims=True) + EPS) * g
        o_ref[rows, :] = (y * scale + shift).astype(o_ref.dtype)


def _rms_mod(x, g, mod, layer, piece_shift, piece_scale, mp, bm=512):
    m, d = x.shape
    n_prompt_tiles = mp // bm

    def pat(piece):
        return lambda i: (layer, jnp.where(i >= n_prompt_tiles, 1, 0), piece)

    return pl.pallas_call(
        _rms_mod_kernel,
        grid=(m // bm,),
        in_specs=[
            pl.BlockSpec((bm, d), lambda i: (i, 0)),
            pl.BlockSpec((1, d), lambda i: (0, 0)),
            pl.BlockSpec((None, PAT, d), pat(piece_shift)),
            pl.BlockSpec((None, PAT, d), pat(piece_scale)),
        ],
        out_specs=pl.BlockSpec((bm, d), lambda i: (i, 0)),
        out_shape=jax.ShapeDtypeStruct((m, d), BF16),
        compiler_params=_params("arbitrary"),
        name="rms_mod",
    )(x, g.reshape(1, d), mod, mod)


def _rms_kernel(x_ref, g_ref, o_ref):
    x = x_ref[...]
    o_ref[...] = x * lax.rsqrt(jnp.mean(x * x, axis=-1, keepdims=True) + EPS) * g_ref[...]


def _rms(x, g, row0, nrows, bm=256):
    d = x.shape[1]
    off = row0 // bm
    return pl.pallas_call(
        _rms_kernel,
        grid=(nrows // bm,),
        in_specs=[pl.BlockSpec((bm, d), lambda i: (i + off, 0)), pl.BlockSpec((1, d), lambda i: (0, 0))],
        out_specs=pl.BlockSpec((bm, d), lambda i: (i, 0)),
        out_shape=jax.ShapeDtypeStruct((nrows, d), F32),
        compiler_params=_params("arbitrary"),
        name="rms_final",
    )(x, g.reshape(1, d))


def _layernorm_kernel(x_ref, g_ref, b_ref, o_ref):
    x = x_ref[...]
    mu = jnp.mean(x, axis=-1, keepdims=True)
    xc = x - mu
    var = jnp.mean(xc * xc, axis=-1, keepdims=True)
    o_ref[...] = (xc * lax.rsqrt(var + EPS) * g_ref[...] + b_ref[...]).astype(o_ref.dtype)


def _layernorm(x, g, b, out_dtype, row0, nrows, bm=256):
    d = x.shape[1]
    off = row0 // bm
    return pl.pallas_call(
        _layernorm_kernel,
        grid=(nrows // bm,),
        in_specs=[
            pl.BlockSpec((bm, d), lambda i: (i + off, 0)),
            pl.BlockSpec((1, d), lambda i: (0, 0)),
            pl.BlockSpec((1, d), lambda i: (0, 0)),
        ],
        out_specs=pl.BlockSpec((bm, d), lambda i: (i, 0)),
        out_shape=jax.ShapeDtypeStruct((nrows, d), out_dtype),
        compiler_params=_params("arbitrary"),
        name="layernorm",
    )(x, g.reshape(1, d), b.reshape(1, d))


def _mm_bias_act_kernel(x_ref, w_ref, b_ref, o_ref, *, act):
    acc = jnp.dot(x_ref[...], w_ref[...].astype(BF16), preferred_element_type=F32) + b_ref[...]
    if act == "gelu":
        acc = jax.nn.gelu(acc)
    o_ref[...] = acc.astype(o_ref.dtype)


def _mm_bias_act(x, w, b, layer, col0, ncols, act, out_dtype, bm=1024, bn=512):
    m, k = x.shape
    noff = col0 // bn
    b3 = b.reshape(b.shape[0], 1, b.shape[1])
    return pl.pallas_call(
        functools.partial(_mm_bias_act_kernel, act=act),
        grid=(m // bm, ncols // bn),
        in_specs=[
            pl.BlockSpec((bm, k), lambda i, n: (i, 0)),
            pl.BlockSpec((None, k, bn), lambda i, n: (layer, 0, n + noff)),
            pl.BlockSpec((None, 1, bn), lambda i, n: (layer, 0, n + noff)),
        ],
        out_specs=pl.BlockSpec((bm, bn), lambda i, n: (i, n)),
        out_shape=jax.ShapeDtypeStruct((m, ncols), out_dtype),
        compiler_params=_params("arbitrary", "arbitrary"),
        name="mm_bias_act",
    )(x, w, b3)


def _mm_swiglu_kernel(x_ref, w1_ref, w3_ref, o_ref):
    x = x_ref[...]
    a = jnp.dot(x, w1_ref[...].astype(BF16), preferred_element_type=F32)
    b = jnp.dot(x, w3_ref[...].astype(BF16), preferred_element_type=F32)
    o_ref[...] = (a * _sigmoid(a) * b).astype(o_ref.dtype)


def _mm_swiglu(x, w1, w3, layer, bm=1024, bn=256):
    m, k = x.shape
    n = w1.shape[2]
    wspec = pl.BlockSpec((None, k, bn), lambda i, j: (layer, 0, j))
    return pl.pallas_call(
        _mm_swiglu_kernel,
        grid=(m // bm, n // bn),
        in_specs=[pl.BlockSpec((bm, k), lambda i, j: (i, 0)), wspec, wspec],
        out_specs=pl.BlockSpec((bm, bn), lambda i, j: (i, j)),
        out_shape=jax.ShapeDtypeStruct((m, n), BF16),
        compiler_params=_params("arbitrary", "arbitrary"),
        name="mm_swiglu",
    )(x, w1, w3)


def _mm_resid_kernel(x_ref, w_ref, res_ref, gate_ref, o_ref):
    acc = jnp.dot(x_ref[...], w_ref[...].astype(BF16), preferred_element_type=F32)
    gate = gate_ref[...]
    for k in range(acc.shape[0] // PAT):
        rows = slice(k * PAT, (k + 1) * PAT)
        o_ref[rows, :] = res_ref[rows, :] + gate * acc[rows, :]


def _mm_resid(x, w, layer_w, res, mod, layer, piece_gate, mp, bm, bn):
    m, k = x.shape
    n = w.shape[2]
    n_prompt_tiles = mp // bm
    gate_col0 = piece_gate * (n // bn)
    return pl.pallas_call(
        _mm_resid_kernel,
        grid=(m // bm, n // bn),
        in_specs=[
            pl.BlockSpec((bm, k), lambda i, j: (i, 0)),
            pl.BlockSpec((None, k, bn), lambda i, j: (layer_w, 0, j)),
            pl.BlockSpec((bm, bn), lambda i, j: (i, j)),
            pl.BlockSpec((None, PAT, bn), lambda i, j: (layer, jnp.where(i >= n_prompt_tiles, 1, 0), gate_col0 + j)),
        ],
        out_specs=pl.BlockSpec((bm, bn), lambda i, j: (i, j)),
        out_shape=jax.ShapeDtypeStruct((m, n), F32),
        input_output_aliases={2: 0},
        compiler_params=_params("arbitrary", "arbitrary"),
        name="mm_resid",
    )(x, w, res, mod)


def _gate_prompt_kernel(u_ref, v_ref, mix_ref, bias_ref, y_ref, *, group):
    for g in range(mix_ref.shape[0]):
        cols = slice(g * group, (g + 1) * group)
        sv = jnp.dot(mix_ref[g], v_ref[:, cols], preferred_element_type=F32) + bias_ref[g]
        y_ref[:, cols] = (u_ref[:, cols].astype(F32) * sv).astype(y_ref.dtype)


def _gate_prompt(u, vn, mix, bias, mp, group, col_tiles=2):
    m, d = u.shape
    n_groups, r, _ = mix.shape
    gpt = n_groups // col_tiles
    bw = gpt * group
    return pl.pallas_call(
        functools.partial(_gate_prompt_kernel, group=group),
        grid=(col_tiles, mp // r),
        in_specs=[
            pl.BlockSpec((r, bw), lambda h, c: (c, h)),
            pl.BlockSpec((r, bw), lambda h, c: (c, h)),
            pl.BlockSpec((gpt, r, r), lambda h, c: (h, 0, 0)),
            pl.BlockSpec((gpt, r, 1), lambda h, c: (h, 0, 0)),
        ],
        out_specs=pl.BlockSpec((r, bw), lambda h, c: (c, h)),
        out_shape=jax.ShapeDtypeStruct((m, d), BF16),
        compiler_params=_params("arbitrary", "arbitrary"),
        name="gate_prompt",
    )(u, vn, mix, bias)


def _gate_sample_kernel(w_ref, b_ref, u_ref, v_ref, y_any_ref, y_ref, *, steps, p):
    del y_any_ref
    g = pl.program_id(0)
    for t in range(steps):
        acc = w_ref[g, t * steps] * v_ref[0:p, :]
        for s in range(1, t + 1):
            acc = acc + w_ref[g, t * steps + s] * v_ref[s * p:(s + 1) * p, :]
        acc = acc + b_ref[g, t]
        rows = slice(t * p, (t + 1) * p)
        y_ref[rows, :] = (u_ref[rows, :].astype(F32) * acc).astype(y_ref.dtype)


def _gate_sample(w_small, b_small, u, vn_s, y, mp, steps, p, group):
    ms = steps * p
    n_groups = w_small.shape[0]
    row_blk = mp // ms
    return pl.pallas_call(
        functools.partial(_gate_sample_kernel, steps=steps, p=p),
        grid=(n_groups,),
        in_specs=[
            pl.BlockSpec(memory_space=pltpu.SMEM),
            pl.BlockSpec(memory_space=pltpu.SMEM),
            pl.BlockSpec((ms, group), lambda g: (row_blk, g)),
            pl.BlockSpec((ms, group), lambda g: (0, g)),
            pl.BlockSpec(memory_space=pl.ANY),
        ],
        out_specs=pl.BlockSpec((ms, group), lambda g: (row_blk, g)),
        out_shape=jax.ShapeDtypeStruct(y.shape, y.dtype),
        input_output_aliases={4: 0},
        compiler_params=_params("arbitrary"),
        name="gate_sample",
    )(w_small, b_small, u, vn_s, y)


def _window_starts(block, blocks_per_group, group_w, win):
    return [min((bl * block) // LANES * LANES, group_w - win) for bl in range(blocks_per_group)]


def _rglru_kernel(*refs, p, halo, starts, win, aliased):
    if aliased:
        refs = refs[1:]
    (xb_ref, gg_ref, hist_ref, h0_ref, wc_ref, bc_ref, wra_ref, bra_ref, wix_ref, bix_ref, lam_ref,
     y_ref, conv_ref, hlast_ref, s_ref, xc_ref, xcb_ref, a_ref, bx_ref, hc_ref) = refs
    t = pl.program_id(1)
    bt, gw = xb_ref.shape

    @pl.when(t == 0)
    def _():
        s_ref[0:halo, :] = hist_ref[...]
        hc_ref[...] = h0_ref[...]

    s_ref[halo:halo + bt, :] = xb_ref[...]
    taps = [halo - (CONV_W - 1 - k) * p for k in range(CONV_W)]
    rc = min(bt, 64)

    def conv(i, carry):
        r0 = pl.multiple_of(i * rc, rc)
        sl = s_ref[pl.ds(r0, rc + halo), :]
        rl = sl if p % SUBLANES == 0 else pltpu.roll(sl, p, 0)
        xc = bc_ref[...]
        for k, lo in enumerate(taps):
            src = sl[lo:lo + rc] if lo % SUBLANES == 0 else rl[lo + p:lo + p + rc]
            xc = xc + wc_ref[k:k + 1, :] * src
        xc_ref[pl.ds(r0, rc), :] = xc
        xcb_ref[pl.ds(r0, rc), :] = xc.astype(BF16)
        return carry

    lax.fori_loop(0, bt // rc, conv, 0)
    s_ref[0:halo, :] = s_ref[bt:bt + halo, :]

    a_ref[...] = jnp.broadcast_to(0.5 * bra_ref[...], (bt, gw))
    bx_ref[...] = jnp.broadcast_to(0.5 * bix_ref[...], (bt, gw))
    for bl, lo in enumerate(starts):
        xw = xcb_ref[:, lo:lo + win]
        a_ref[:, lo:lo + win] += jnp.dot(xw, wra_ref[bl], preferred_element_type=F32)
        bx_ref[:, lo:lo + win] += jnp.dot(xw, wix_ref[bl], preferred_element_type=F32)

    lam = lam_ref[...]
    c8h = (-0.5 * LRU_C) * (jnp.maximum(-lam, 0.0) + jnp.log1p(jnp.exp(-jnp.abs(lam))))

    rg = min(bt, 32)

    def gates(i, carry):
        rows = pl.ds(pl.multiple_of(i * rg, rg), rg)
        log_a = c8h * jnp.tanh(a_ref[rows, :]) + c8h
        a = jnp.exp(log_a)
        mult = jnp.sqrt(jnp.tanh(log_a) * (-1.0 - a * a))
        a_ref[rows, :] = a
        bx_ref[rows, :] = mult * (xc_ref[rows, :] * (0.5 * jnp.tanh(bx_ref[rows, :]) + 0.5))
        return carry

    lax.fori_loop(0, bt // rg, gates, 0)

    if p % SUBLANES == 0:
        h = hc_ref[...]
        for step in range(bt // p):
            rows = slice(step * p, (step + 1) * p)
            h = a_ref[rows, :] * h + bx_ref[rows, :]
            y_ref[rows, :] = (gg_ref[rows, :].astype(F32) * h).astype(y_ref.dtype)
        hc_ref[...] = h
    else:
        upper = lax.broadcasted_iota(jnp.int32, (SUBLANES, gw), 0) >= p

        def scan(i, hc):
            rows = pl.ds(pl.multiple_of(i * SUBLANES, SUBLANES), SUBLANES)
            a8 = a_ref[rows, :]
            b8 = bx_ref[rows, :]
            h_lo = a8 * hc + b8
            h_hi = a8 * pltpu.roll(h_lo, p, 0) + b8
            a_ref[rows, :] = jnp.where(upper, h_hi, h_lo)
            return jnp.where(upper, h_hi, pltpu.roll(h_hi, p, 0))

        hc_ref[...] = lax.fori_loop(0, bt // SUBLANES, scan, hc_ref[...], unroll=2)

        def gate_out(i, carry):
            rows = pl.ds(pl.multiple_of(i * rg, rg), rg)
            y_ref[rows, :] = (gg_ref[rows, :].astype(F32) * a_ref[rows, :]).astype(y_ref.dtype)
            return carry

        lax.fori_loop(0, bt // rg, gate_out, 0)

    @pl.when(t == pl.num_programs(1) - 1)
    def _():
        conv_ref[...] = s_ref[0:halo, :]
        hlast_ref[...] = hc_ref[...]


def _rglru(xb, gg, hist, h0, wconv, bconv, wra, bra, wix, bix, lam, y_prev, *, row0, nrows, p, bt,
           block, blocks_per_group, win):
    m, c = xb.shape
    gw = block * blocks_per_group
    n_groups = c // gw
    halo = hist.shape[0]
    hrows = h0.shape[0]
    starts = _window_starts(block, blocks_per_group, gw, win)
    rb = row0 // bt
    aliased = y_prev is not None

    row_blk = lambda g, t: (rb + t, g)
    vec = pl.BlockSpec((1, gw), lambda g, t: (0, g))
    wspec = pl.BlockSpec((blocks_per_group, win, win), lambda g, t: (g, 0, 0))
    in_specs = [
        pl.BlockSpec((bt, gw), row_blk),
        pl.BlockSpec((bt, gw), row_blk),
        pl.BlockSpec((halo, gw), lambda g, t: (0, g)),
        pl.BlockSpec((hrows, gw), lambda g, t: (0, g)),
        pl.BlockSpec((CONV_W, gw), lambda g, t: (0, g)),
        vec, wspec, vec, wspec, vec, vec,
    ]
    args = [xb, gg, hist, h0, wconv, bconv.reshape(1, c), wra, bra.reshape(1, c), wix, bix.reshape(1, c),
            lam.reshape(1, c)]
    aliases = {}
    if aliased:
        in_specs = [pl.BlockSpec(memory_space=pl.ANY)] + in_specs
        args = [y_prev] + args
        aliases = {0: 0}
    return pl.pallas_call(
        functools.partial(_rglru_kernel, p=p, halo=halo, starts=starts, win=win, aliased=aliased),
        grid=(n_groups, nrows // bt),
        in_specs=in_specs,
        out_specs=[
            pl.BlockSpec((bt, gw), row_blk),
            pl.BlockSpec((halo, gw), lambda g, t: (0, g)),
            pl.BlockSpec((hrows, gw), lambda g, t: (0, g)),
        ],
        out_shape=[
            jax.ShapeDtypeStruct((m, c), BF16),
            jax.ShapeDtypeStruct((halo, c), F32),
            jax.ShapeDtypeStruct((hrows, c), F32),
        ],
        scratch_shapes=[
            pltpu.VMEM((halo + bt, gw), F32),
            pltpu.VMEM((bt, gw), F32),
            pltpu.VMEM((bt, gw), BF16),
            pltpu.VMEM((bt, gw), F32),
            pltpu.VMEM((bt, gw), F32),
            pltpu.VMEM((hrows, gw), F32),
        ],
        input_output_aliases=aliases,
        compiler_params=_params("arbitrary", "arbitrary"),
        name="rglru",
    )(*args)


def _pad_block_weights(w, block, blocks_per_group, win):
    gw = block * blocks_per_group
    starts = _window_starts(block, blocks_per_group, gw, win)
    n_groups = w.shape[0] // blocks_per_group
    wg = w.astype(BF16).reshape(n_groups, blocks_per_group, block, block)
    padded = []
    for bl, lo in enumerate(starts):
        off = bl * block - lo
        rest = win - off - block
        padded.append(jnp.pad(wg[:, bl], ((0, 0), (off, rest), (off, rest))))
    return jnp.stack(padded, axis=1).reshape(w.shape[0], win, win)


def kernel(x_prompt, x_sample, state_conv, state_h, c_prompt, c_sample, w_ada, b_ada, g_norm1, g_norm2, g_final,
           w_in_a, b_in_a, g_v_a, b_v_a, w_s_a, b_s_a, w_out_a, w_in_b, b_in_b, w_conv_b, b_conv_b, w_ra_b, b_ra_b,
           w_ix_b, b_ix_b, lam_b, w_out_b, w_ff1, w_ff3, w_ff2):
    nb, t_p, d = x_prompt.shape
    ns, t_s, _ = x_sample.shape
    depth = w_ada.shape[0]
    d_a = w_out_a.shape[1]
    d_rnn = w_out_b.shape[1]
    group = d_a // N_GROUPS_A
    block = d_rnn // N_BLOCKS_B
    mp, ms = nb * t_p, ns * t_s
    assert ns == PAT and PAT % nb == 0 and t_p % CHUNK == 0 and t_s <= CHUNK and nb == 4

    x = jnp.concatenate([x_prompt.transpose(1, 0, 2).reshape(mp, d), x_sample.transpose(1, 0, 2).reshape(ms, d)])
    c_pat = jnp.concatenate([jnp.tile(c_prompt, (PAT // nb, 1)), c_sample])
    mod = _ada(c_pat, w_ada, b_ada)

    bf = lambda w: w.astype(BF16)
    w_out_b, w_ff2 = bf(w_out_b), bf(w_ff2)

    blocks_per_group = LANES // math.gcd(block, LANES)
    win = (-(-block // LANES) + 1) * LANES
    halo_p = -(-(CONV_W - 1) * nb // SUBLANES) * SUBLANES

    v_new, conv_p, h_p, conv_s, h_s = [], [], [], [], []
    for i in range(depth):
        j = i // 2
        h = _rms_mod(x, g_norm1[i], mod, i, 0, 1, mp)
        if i % 2 == 0:
            u = _mm_bias_act(h, w_in_a, b_in_a, j, 0, d_a, "gelu", BF16)
            v = _mm_bias_act(h, w_in_a, b_in_a, j, d_a, d_a, "gelu", F32)
            vn = _layernorm(v, g_v_a[j], b_v_a[j], BF16, 0, mp)
            vn_s = _layernorm(v, g_v_a[j], b_v_a[j], F32, mp, ms)
            w_tril = jnp.where(jnp.tril(jnp.ones((CHUNK, CHUNK), bool)), w_s_a[j], 0.0)
            mix = (bf(w_tril)[:, :, None, :, None] * jnp.eye(nb, dtype=BF16)[None, None, :, None, :]).reshape(
                N_GROUPS_A, CHUNK * nb, CHUNK * nb)
            bias = jnp.repeat(b_s_a[j], nb, axis=1)[:, :, None]
            y = _gate_prompt(u, vn, mix, bias, mp, group)
            w_small = jnp.where(jnp.tril(jnp.ones((t_s, t_s), bool)), w_s_a[j][:, :t_s, :t_s], 0.0)
            y = _gate_sample(w_small.reshape(N_GROUPS_A, t_s * t_s), b_s_a[j][:, :t_s], u, vn_s, y, mp, t_s, ns, group)
            v_new.append(vn_s.reshape(t_s, ns, d_a).transpose(1, 0, 2))
            x = _mm_resid(y, w_out_a, j, x, mod, i, 2, mp, 1024, 512)
        else:
            gg = _mm_bias_act(h, w_in_b, b_in_b, j, 0, d_rnn, "gelu", BF16)
            xb = _mm_bias_act(h, w_in_b, b_in_b, j, d_rnn, d_rnn, None, F32)
            wra = _pad_block_weights(0.5 * w_ra_b[j], block, blocks_per_group, win)
            wix = _pad_block_weights(0.5 * w_ix_b[j], block, blocks_per_group, win)
            common = (w_conv_b[j], b_conv_b[j], wra, b_ra_b[j], wix, b_ix_b[j], lam_b[j])
            geom = dict(block=block, blocks_per_group=blocks_per_group, win=win)
            y, cp, hp = _rglru(xb, gg, jnp.zeros((halo_p, d_rnn), F32), jnp.zeros((SUBLANES, d_rnn), F32), *common,
                               None, row0=0, nrows=mp, p=nb, bt=512, **geom)
            hist_s = state_conv[j].transpose(1, 0, 2).reshape((CONV_W - 1) * ns, d_rnn)
            y, cs, hs = _rglru(xb, gg, hist_s, state_h[j], *common, y, row0=mp, nrows=ms, p=ns, bt=ns, **geom)
            conv_p.append(cp[halo_p - (CONV_W - 1) * nb:].reshape(CONV_W - 1, nb, d_rnn).transpose(1, 0, 2))
            h_p.append(hp[SUBLANES - nb:])
            conv_s.append(cs.reshape(CONV_W - 1, ns, d_rnn).transpose(1, 0, 2))
            h_s.append(hs)
            x = _mm_resid(y, w_out_b, j, x, mod, i, 2, mp, 1024, 512)
        h = _rms_mod(x, g_norm2[i], mod, i, 3, 4, mp)
        f = _mm_swiglu(h, w_ff1, w_ff3, i)
        x = _mm_resid(f, w_ff2, i, x, mod, i, 5, mp, 512, 512)

    y_prompt = _rms(x, g_final, 0, mp).reshape(t_p, nb, d).transpose(1, 0, 2)
    y_sample = _rms(x, g_final, mp, ms).reshape(t_s, ns, d).transpose(1, 0, 2)
    return (y_prompt, y_sample, jnp.stack(v_new), jnp.stack(conv_p), jnp.stack(h_p), jnp.stack(conv_s),
            jnp.stack(h_s))
```

```python
import functools
import math

import jax
import jax.numpy as jnp
from jax import lax
from jax.experimental import pallas as pl
from jax.experimental.pallas import tpu as pltpu

EPS = 1e-6
LRU_C = 8.0
CHUNK = 128
N_GROUPS_A = 16
N_BLOCKS_B = 16
CONV_W = 4

LANES = 128
SUBLANES = 8
PAT = 128
VMEM_LIMIT_BYTES = 56 * 1024 * 1024

F32 = jnp.float32
BF16 = jnp.bfloat16


def _params(*sem):
    return pltpu.CompilerParams(dimension_semantics=sem, vmem_limit_bytes=VMEM_LIMIT_BYTES)


def _sigmoid(x):
    return 0.5 * (jnp.tanh(0.5 * x) + 1.0)


def _ada_kernel(c_ref, w_ref, b_ref, o_ref, act_ref):
    @pl.when((pl.program_id(0) == 0) & (pl.program_id(1) == 0))
    def _():
        c = c_ref[...]
        act_ref[...] = (c * _sigmoid(c)).astype(BF16)

    w = w_ref[...].astype(BF16)
    o_ref[...] = jnp.dot(act_ref[...], w, preferred_element_type=F32) + b_ref[...]


def _ada(c_pat, w_ada, b_ada, bn=512):
    depth, d, n6 = w_ada.shape
    rows = c_pat.shape[0]
    return pl.pallas_call(
        _ada_kernel,
        grid=(depth, n6 // bn),
        in_specs=[
            pl.BlockSpec((rows, d), lambda i, n: (0, 0)),
            pl.BlockSpec((None, d, bn), lambda i, n: (i, 0, n)),
            pl.BlockSpec((None, 1, bn), lambda i, n: (i, 0, n)),
        ],
        out_specs=pl.BlockSpec((None, rows, bn), lambda i, n: (i, 0, n)),
        out_shape=jax.ShapeDtypeStruct((depth, rows, n6), F32),
        scratch_shapes=[pltpu.VMEM((rows, d), BF16)],
        compiler_params=_params("arbitrary", "arbitrary"),
        name="ada",
    )(c_pat, w_ada, b_ada.reshape(depth, 1, n6))


def _rms_mod_kernel(x_ref, g_ref, sh_ref, sc_ref, o_ref):
    g = g_ref[...]
    scale = 1.0 + sc_ref[...]
    shift = sh_ref[...]
    for k in range(x_ref.shape[0] // PAT):
        rows = slice(k * PAT, (k + 1) * PAT)
        x = x_ref[rows, :]
        y = x * lax.rsqrt(jnp.mean(x * x, axis=-1, keepdims=True) + EPS) * g
        o_ref[rows, :] = (y * scale + shift).astype(o_ref.dtype)


def _rms_mod(x, g, mod, layer, piece_shift, piece_scale, mp, bm=512):
    m, d = x.shape
    n_prompt_tiles = mp // bm

    def pat(piece):
        return lambda i: (layer, jnp.where(i >= n_prompt_tiles, 1, 0), piece)

    return pl.pallas_call(
        _rms_mod_kernel,
        grid=(m // bm,),
        in_specs=[
            pl.BlockSpec((bm, d), lambda i: (i, 0)),
            pl.BlockSpec((1, d), lambda i: (0, 0)),
            pl.BlockSpec((None, PAT, d), pat(piece_shift)),
            pl.BlockSpec((None, PAT, d), pat(piece_scale)),
        ],
        out_specs=pl.BlockSpec((bm, d), lambda i: (i, 0)),
        out_shape=jax.ShapeDtypeStruct((m, d), BF16),
        compiler_params=_params("arbitrary"),
        name="rms_mod",
    )(x, g.reshape(1, d), mod, mod)


def _rms_kernel(x_ref, g_ref, o_ref):
    x = x_ref[...]
    o_ref[...] = x * lax.rsqrt(jnp.mean(x * x, axis=-1, keepdims=True) + EPS) * g_ref[...]


def _rms(x, g, row0, nrows, bm=256):
    d = x.shape[1]
    off = row0 // bm
    return pl.pallas_call(
        _rms_kernel,
        grid=(nrows // bm,),
        in_specs=[pl.BlockSpec((bm, d), lambda i: (i + off, 0)), pl.BlockSpec((1, d), lambda i: (0, 0))],
        out_specs=pl.BlockSpec((bm, d), lambda i: (i, 0)),
        out_shape=jax.ShapeDtypeStruct((nrows, d), F32),
        compiler_params=_params("arbitrary"),
        name="rms_final",
    )(x, g.reshape(1, d))


def _rms_seq_major_kernel(x_ref, g_ref, o_ref, s_ref):
    nb, bt, d = o_ref.shape
    x = x_ref[...]
    y = x * lax.rsqrt(jnp.mean(x * x, axis=-1, keepdims=True) + EPS) * g_ref[...]
    for c in range(d // LANES):
        s_ref[c] = y[:, c * LANES:(c + 1) * LANES]
    for c in range(d // LANES):
        for s in range(nb):
            o_ref[s, :, c * LANES:(c + 1) * LANES] = s_ref[c, pl.ds(s, bt, stride=nb), :]


def _rms_seq_major(x, g, nb, t_len, bt=128):
    d = x.shape[1]
    return pl.pallas_call(
        _rms_seq_major_kernel,
        grid=(t_len // bt,),
        in_specs=[pl.BlockSpec((bt * nb, d), lambda i: (i, 0)), pl.BlockSpec((1, d), lambda i: (0, 0))],
        out_specs=pl.BlockSpec((nb, bt, d), lambda i: (0, i, 0)),
        out_shape=jax.ShapeDtypeStruct((nb, t_len, d), F32),
        scratch_shapes=[pltpu.VMEM((d // LANES, bt * nb, LANES), F32)],
        compiler_params=_params("arbitrary"),
        name="rms_final_prompt",
    )(x, g.reshape(1, d))


def _to_time_major_kernel(xp_ref, xs_ref, o_ref, s_ref, *, n_prompt_tiles):
    nb, bt, d = xp_ref.shape
    i = pl.program_id(0)

    @pl.when(i < n_prompt_tiles)
    def _():
        for c in range(d // LANES):
            for s in range(nb):
                s_ref[c, pl.ds(s, bt, stride=nb), :] = xp_ref[s, :, c * LANES:(c + 1) * LANES]
        for c in range(d // LANES):
            o_ref[:, c * LANES:(c + 1) * LANES] = s_ref[c]

    @pl.when(i >= n_prompt_tiles)
    def _():
        o_ref[...] = xs_ref[...]


def _to_time_major(x_prompt, xs_tm, bt=128):
    nb, t_len, d = x_prompt.shape
    ms = xs_tm.shape[0]
    rows = bt * nb
    n_prompt_tiles = t_len // bt
    return pl.pallas_call(
        functools.partial(_to_time_major_kernel, n_prompt_tiles=n_prompt_tiles),
        grid=(n_prompt_tiles + ms // rows,),
        in_specs=[
            pl.BlockSpec((nb, bt, d), lambda i: (0, jnp.minimum(i, n_prompt_tiles - 1), 0)),
            pl.BlockSpec((rows, d), lambda i: (jnp.maximum(i - n_prompt_tiles, 0), 0)),
        ],
        out_specs=pl.BlockSpec((rows, d), lambda i: (i, 0)),
        out_shape=jax.ShapeDtypeStruct((nb * t_len + ms, d), F32),
        scratch_shapes=[pltpu.VMEM((d // LANES, rows, LANES), F32)],
        compiler_params=_params("arbitrary"),
        name="to_time_major",
    )(x_prompt, xs_tm)


def _layernorm_kernel(x_ref, g_ref, b_ref, o_ref):
    x = x_ref[...]
    mu = jnp.mean(x, axis=-1, keepdims=True)
    xc = x - mu
    var = jnp.mean(xc * xc, axis=-1, keepdims=True)
    o_ref[...] = (xc * lax.rsqrt(var + EPS) * g_ref[...] + b_ref[...]).astype(o_ref.dtype)


def _layernorm(x, g, b, out_dtype, row0, nrows, bm=256):
    d = x.shape[1]
    off = row0 // bm
    return pl.pallas_call(
        _layernorm_kernel,
        grid=(nrows // bm,),
        in_specs=[
            pl.BlockSpec((bm, d), lambda i: (i + off, 0)),
            pl.BlockSpec((1, d), lambda i: (0, 0)),
            pl.BlockSpec((1, d), lambda i: (0, 0)),
        ],
        out_specs=pl.BlockSpec((bm, d), lambda i: (i, 0)),
        out_shape=jax.ShapeDtypeStruct((nrows, d), out_dtype),
        compiler_params=_params("arbitrary"),
        name="layernorm",
    )(x, g.reshape(1, d), b.reshape(1, d))


def _mm_bias_act_kernel(x_ref, w_ref, b_ref, o_ref, *, act):
    acc = jnp.dot(x_ref[...], w_ref[...].astype(BF16), preferred_element_type=F32) + b_ref[...]
    if act == "gelu":
        acc = jax.nn.gelu(acc)
    o_ref[...] = acc.astype(o_ref.dtype)


def _mm_bias_act(x, w, b, layer, col0, ncols, act, out_dtype, bm=1024, bn=512):
    m, k = x.shape
    noff = col0 // bn
    b3 = b.reshape(b.shape[0], 1, b.shape[1])
    return pl.pallas_call(
        functools.partial(_mm_bias_act_kernel, act=act),
        grid=(m // bm, ncols // bn),
        in_specs=[
            pl.BlockSpec((bm, k), lambda i, n: (i, 0)),
            pl.BlockSpec((None, k, bn), lambda i, n: (layer, 0, n + noff)),
            pl.BlockSpec((None, 1, bn), lambda i, n: (layer, 0, n + noff)),
        ],
        out_specs=pl.BlockSpec((bm, bn), lambda i, n: (i, n)),
        out_shape=jax.ShapeDtypeStruct((m, ncols), out_dtype),
        compiler_params=_params("arbitrary", "arbitrary"),
        name="mm_bias_act",
    )(x, w, b3)


def _mm_swiglu_kernel(x_ref, w1_ref, w3_ref, w2_ref, o_ref, w2b_ref):
    x = x_ref[...]
    a = jnp.dot(x, w1_ref[...].astype(BF16), preferred_element_type=F32)
    b = jnp.dot(x, w3_ref[...].astype(BF16), preferred_element_type=F32)
    o_ref[...] = (a * _sigmoid(a) * b).astype(o_ref.dtype)

    @pl.when(pl.program_id(0) == 0)
    def _():
        w2b_ref[...] = w2_ref[...].astype(BF16)


def _mm_swiglu(x, w1, w3, w2, layer, bm=1024, bn=256):
    m, k = x.shape
    n = w1.shape[2]
    nj = n // bn
    rk, n2 = w2.shape[1] // nj, w2.shape[2]
    wspec = pl.BlockSpec((None, k, bn), lambda i, j: (layer, 0, j))
    chunk = lambda i, j: jnp.where(i == 0, j, nj - 1)
    return pl.pallas_call(
        _mm_swiglu_kernel,
        grid=(m // bm, nj),
        in_specs=[
            pl.BlockSpec((bm, k), lambda i, j: (i, 0)), wspec, wspec,
            pl.BlockSpec((None, rk, n2), lambda i, j: (layer, chunk(i, j), 0)),
        ],
        out_specs=[
            pl.BlockSpec((bm, bn), lambda i, j: (i, j)),
            pl.BlockSpec((rk, n2), lambda i, j: (chunk(i, j), 0)),
        ],
        out_shape=[jax.ShapeDtypeStruct((m, n), BF16), jax.ShapeDtypeStruct((nj * rk, n2), BF16)],
        compiler_params=_params("arbitrary", "arbitrary"),
        name="mm_swiglu",
    )(x, w1, w3, w2)


def _mm_resid_kernel(x_ref, w_ref, res_ref, gate_ref, o_ref):
    acc = jnp.dot(x_ref[...], w_ref[...].astype(BF16), preferred_element_type=F32)
    gate = gate_ref[...]
    for k in range(acc.shape[0] // PAT):
        rows = slice(k * PAT, (k + 1) * PAT)
        o_ref[rows, :] = res_ref[rows, :] + gate * acc[rows, :]


def _mm_resid(x, w, layer_w, res, mod, layer, piece_gate, mp, bm, bn):
    m, k = x.shape
    n = w.shape[2]
    n_prompt_tiles = mp // bm
    gate_col0 = piece_gate * (n // bn)
    return pl.pallas_call(
        _mm_resid_kernel,
        grid=(m // bm, n // bn),
        in_specs=[
            pl.BlockSpec((bm, k), lambda i, j: (i, 0)),
            pl.BlockSpec((None, k, bn), lambda i, j: (layer_w, 0, j)),
            pl.BlockSpec((bm, bn), lambda i, j: (i, j)),
            pl.BlockSpec((None, PAT, bn), lambda i, j: (layer, jnp.where(i >= n_prompt_tiles, 1, 0), gate_col0 + j)),
        ],
        out_specs=pl.BlockSpec((bm, bn), lambda i, j: (i, j)),
        out_shape=jax.ShapeDtypeStruct((m, n), F32),
        input_output_aliases={2: 0},
        compiler_params=_params("arbitrary", "arbitrary"),
        name="mm_resid",
    )(x, w, res, mod)


def _gate_prompt_kernel(u_ref, v_ref, w_ref, bias_ref, y_ref, mix_ref, *, group, nb, n_chunks):
    c = pl.program_id(1)
    gpt, r, _ = mix_ref.shape
    chunk = w_ref.shape[1]

    @pl.when(c == 0)
    def _():
        shift = nb.bit_length() - 1
        rep = (lax.shift_right_logical(lax.broadcasted_iota(jnp.int32, (r, chunk), 0), shift)
               == lax.broadcasted_iota(jnp.int32, (r, chunk), 1)).astype(BF16)
        causal = lax.broadcasted_iota(jnp.int32, (chunk, chunk), 0) >= lax.broadcasted_iota(jnp.int32, (chunk, chunk), 1)
        same_seq = ((lax.broadcasted_iota(jnp.int32, (r, r), 0) & (nb - 1))
                    == (lax.broadcasted_iota(jnp.int32, (r, r), 1) & (nb - 1)))
        for g in range(gpt):
            wg = jnp.where(causal, w_ref[g], 0.0).astype(BF16)
            left = jnp.dot(rep, wg, preferred_element_type=F32).astype(BF16)
            full = lax.dot_general(left, rep, (((1,), (1,)), ((), ())), preferred_element_type=F32)
            mix_ref[g] = jnp.where(same_seq, full, 0.0).astype(BF16)

    @pl.when(c < n_chunks)
    def _():
        for g in range(gpt):
            cols = slice(g * group, (g + 1) * group)
            sv = jnp.dot(mix_ref[g], v_ref[:, cols], preferred_element_type=F32) + bias_ref[g]
            y_ref[:, cols] = (u_ref[:, cols].astype(F32) * sv).astype(y_ref.dtype)

    @pl.when(c >= n_chunks)
    def _():
        y_ref[...] = jnp.zeros(y_ref.shape, y_ref.dtype)


def _gate_prompt(u, vn, w_s, bias, mp, group, nb, col_tiles=2):
    m, d = u.shape
    n_groups, chunk, _ = w_s.shape
    r = chunk * nb
    gpt = n_groups // col_tiles
    bw = gpt * group
    n_chunks = mp // r
    return pl.pallas_call(
        functools.partial(_gate_prompt_kernel, group=group, nb=nb, n_chunks=n_chunks),
        grid=(col_tiles, m // r),
        in_specs=[
            pl.BlockSpec((r, bw), lambda h, c: (c, h)),
            pl.BlockSpec((r, bw), lambda h, c: (jnp.minimum(c, n_chunks - 1), h)),
            pl.BlockSpec((gpt, chunk, chunk), lambda h, c: (h, 0, 0)),
            pl.BlockSpec((gpt, r, 1), lambda h, c: (h, 0, 0)),
        ],
        out_specs=pl.BlockSpec((r, bw), lambda h, c: (c, h)),
        out_shape=jax.ShapeDtypeStruct((m, d), BF16),
        scratch_shapes=[pltpu.VMEM((gpt, r, r), BF16)],
        compiler_params=_params("arbitrary", "arbitrary"),
        name="gate_prompt",
    )(u, vn, w_s, bias)


def _gate_sample_kernel(w_ref, b_ref, u_ref, v_ref, y_any_ref, y_ref, *, steps, p):
    del y_any_ref
    g = pl.program_id(0)
    for t in range(steps):
        acc = w_ref[g, t * steps] * v_ref[0:p, :]
        for s in range(1, t + 1):
            acc = acc + w_ref[g, t * steps + s] * v_ref[s * p:(s + 1) * p, :]
        acc = acc + b_ref[g, t]
        rows = slice(t * p, (t + 1) * p)
        y_ref[rows, :] = (u_ref[rows, :].astype(F32) * acc).astype(y_ref.dtype)


def _gate_sample(w_small, b_small, u, vn_s, y, mp, steps, p, group):
    ms = steps * p
    n_groups = w_small.shape[0]
    row_blk = mp // ms
    return pl.pallas_call(
        functools.partial(_gate_sample_kernel, steps=steps, p=p),
        grid=(n_groups,),
        in_specs=[
            pl.BlockSpec(memory_space=pltpu.SMEM),
            pl.BlockSpec(memory_space=pltpu.SMEM),
            pl.BlockSpec((ms, group), lambda g: (row_blk, g)),
            pl.BlockSpec((ms, group), lambda g: (0, g)),
            pl.BlockSpec(memory_space=pl.ANY),
        ],
        out_specs=pl.BlockSpec((ms, group), lambda g: (row_blk, g)),
        out_shape=jax.ShapeDtypeStruct(y.shape, y.dtype),
        input_output_aliases={4: 0},
        compiler_params=_params("arbitrary"),
        name="gate_sample",
    )(w_small, b_small, u, vn_s, y)


def _window_starts(block, blocks_per_group, group_w, win):
    return [min((bl * block) // LANES * LANES, group_w - win) for bl in range(blocks_per_group)]


def _rglru_kernel(*refs, n_t, **static):
    y_ref = refs[-(_RGLRU_N_OUT + _RGLRU_N_SCRATCH)]
    t = pl.program_id(1)
    pl.when(t < n_t)(functools.partial(_rglru_tile, *refs, n_t=n_t, **static))

    @pl.when(t >= n_t)
    def _():
        y_ref[...] = jnp.zeros(y_ref.shape, y_ref.dtype)


_RGLRU_N_OUT = 3
_RGLRU_N_SCRATCH = 6


def _rglru_tile(*refs, p, halo, starts, win, aliased, n_t):
    if aliased:
        refs = refs[1:]
    (xb_ref, gg_ref, hist_ref, h0_ref, wc_ref, bc_ref, wra_ref, bra_ref, wix_ref, bix_ref, lam_ref,
     y_ref, conv_ref, hlast_ref, s_ref, xc_ref, xcb_ref, a_ref, bx_ref, hc_ref) = refs
    t = pl.program_id(1)
    bt, gw = xb_ref.shape

    @pl.when(t == 0)
    def _():
        s_ref[0:halo, :] = hist_ref[...]
        hc_ref[...] = h0_ref[...]

    s_ref[halo:halo + bt, :] = xb_ref[...]
    taps = [halo - (CONV_W - 1 - k) * p for k in range(CONV_W)]
    rc = min(bt, 64)

    def conv(i, carry):
        r0 = pl.multiple_of(i * rc, rc)
        sl = s_ref[pl.ds(r0, rc + halo), :]
        rl = sl if p % SUBLANES == 0 else pltpu.roll(sl, p, 0)
        xc = bc_ref[...]
        for k, lo in enumerate(taps):
            src = sl[lo:lo + rc] if lo % SUBLANES == 0 else rl[lo + p:lo + p + rc]
            xc = xc + wc_ref[k:k + 1, :] * src
        xc_ref[pl.ds(r0, rc), :] = xc
        xcb_ref[pl.ds(r0, rc), :] = xc.astype(BF16)
        return carry

    lax.fori_loop(0, bt // rc, conv, 0)
    s_ref[0:halo, :] = s_ref[bt:bt + halo, :]

    a_ref[...] = jnp.broadcast_to(0.5 * bra_ref[...], (bt, gw))
    bx_ref[...] = jnp.broadcast_to(0.5 * bix_ref[...], (bt, gw))
    for bl, lo in enumerate(starts):
        xw = xcb_ref[:, lo:lo + win]
        a_ref[:, lo:lo + win] += jnp.dot(xw, wra_ref[bl], preferred_element_type=F32)
        bx_ref[:, lo:lo + win] += jnp.dot(xw, wix_ref[bl], preferred_element_type=F32)

    lam = lam_ref[...]
    c8h = (-0.5 * LRU_C) * (jnp.maximum(-lam, 0.0) + jnp.log1p(jnp.exp(-jnp.abs(lam))))

    rg = min(bt, 32)

    def gates(i, carry):
        rows = pl.ds(pl.multiple_of(i * rg, rg), rg)
        log_a = c8h * jnp.tanh(a_ref[rows, :]) + c8h
        a = jnp.exp(log_a)
        mult = jnp.sqrt(jnp.tanh(log_a) * (-1.0 - a * a))
        a_ref[rows, :] = a
        bx_ref[rows, :] = mult * (xc_ref[rows, :] * (0.5 * jnp.tanh(bx_ref[rows, :]) + 0.5))
        return carry

    lax.fori_loop(0, bt // rg, gates, 0)

    if p % SUBLANES == 0:
        h = hc_ref[...]
        for step in range(bt // p):
            rows = slice(step * p, (step + 1) * p)
            h = a_ref[rows, :] * h + bx_ref[rows, :]
            y_ref[rows, :] = (gg_ref[rows, :].astype(F32) * h).astype(y_ref.dtype)
        hc_ref[...] = h
    else:
        upper = lax.broadcasted_iota(jnp.int32, (SUBLANES, gw), 0) >= p

        def scan(i, hc):
            rows = pl.ds(pl.multiple_of(i * SUBLANES, SUBLANES), SUBLANES)
            a8 = a_ref[rows, :]
            b8 = bx_ref[rows, :]
            h_lo = a8 * hc + b8
            h_hi = a8 * pltpu.roll(h_lo, p, 0) + b8
            a_ref[rows, :] = jnp.where(upper, h_hi, h_lo)
            return jnp.where(upper, h_hi, pltpu.roll(h_hi, p, 0))

        hc_ref[...] = lax.fori_loop(0, bt // SUBLANES, scan, hc_ref[...], unroll=2)

        def gate_out(i, carry):
            rows = pl.ds(pl.multiple_of(i * rg, rg), rg)
            y_ref[rows, :] = (gg_ref[rows, :].astype(F32) * a_ref[rows, :]).astype(y_ref.dtype)
            return carry

        lax.fori_loop(0, bt // rg, gate_out, 0)

    @pl.when(t == n_t - 1)
    def _():
        conv_ref[...] = s_ref[0:halo, :]
        hlast_ref[...] = hc_ref[...]


def _rglru(xb, gg, hist, h0, wconv, bconv, wra, bra, wix, bix, lam, y_prev, *, row0, nrows, p, bt,
           block, blocks_per_group, win):
    m, c = xb.shape
    gw = block * blocks_per_group
    n_groups = c // gw
    halo = hist.shape[0]
    hrows = h0.shape[0]
    starts = _window_starts(block, blocks_per_group, gw, win)
    rb = row0 // bt
    aliased = y_prev is not None
    n_t = nrows // bt
    n_zero = 0 if aliased else (m - row0 - nrows) // bt

    row_blk = lambda g, t: (rb + t, g)
    in_blk = lambda g, t: (rb + jnp.minimum(t, n_t - 1), g)
    vec = pl.BlockSpec((1, gw), lambda g, t: (0, g))
    wspec = pl.BlockSpec((blocks_per_group, win, win), lambda g, t: (g, 0, 0))
    in_specs = [
        pl.BlockSpec((bt, gw), in_blk),
        pl.BlockSpec((bt, gw), in_blk),
        pl.BlockSpec((halo, gw), lambda g, t: (0, g)),
        pl.BlockSpec((hrows, gw), lambda g, t: (0, g)),
        pl.BlockSpec((CONV_W, gw), lambda g, t: (0, g)),
        vec, wspec, vec, wspec, vec, vec,
    ]
    args = [xb, gg, hist, h0, wconv, bconv.reshape(1, c), wra, bra.reshape(1, c), wix, bix.reshape(1, c),
            lam.reshape(1, c)]
    aliases = {}
    if aliased:
        in_specs = [pl.BlockSpec(memory_space=pl.ANY)] + in_specs
        args = [y_prev] + args
        aliases = {0: 0}
    return pl.pallas_call(
        functools.partial(_rglru_kernel, p=p, halo=halo, starts=starts, win=win, aliased=aliased, n_t=n_t),
        grid=(n_groups, n_t + n_zero),
        in_specs=in_specs,
        out_specs=[
            pl.BlockSpec((bt, gw), row_blk),
            pl.BlockSpec((halo, gw), lambda g, t: (0, g)),
            pl.BlockSpec((hrows, gw), lambda g, t: (0, g)),
        ],
        out_shape=[
            jax.ShapeDtypeStruct((m, c), BF16),
            jax.ShapeDtypeStruct((halo, c), F32),
            jax.ShapeDtypeStruct((hrows, c), F32),
        ],
        scratch_shapes=[
            pltpu.VMEM((halo + bt, gw), F32),
            pltpu.VMEM((bt, gw), F32),
            pltpu.VMEM((bt, gw), BF16),
            pltpu.VMEM((bt, gw), F32),
            pltpu.VMEM((bt, gw), F32),
            pltpu.VMEM((hrows, gw), F32),
        ],
        input_output_aliases=aliases,
        compiler_params=_params("arbitrary", "arbitrary"),
        name="rglru",
    )(*args)


def _pad_block_weights(w, block, blocks_per_group, win):
    gw = block * blocks_per_group
    starts = _window_starts(block, blocks_per_group, gw, win)
    n_groups = w.shape[0] // blocks_per_group
    wg = w.astype(BF16).reshape(n_groups, blocks_per_group, block, block)
    padded = []
    for bl, lo in enumerate(starts):
        off = bl * block - lo
        rest = win - off - block
        padded.append(jnp.pad(wg[:, bl], ((0, 0), (off, rest), (off, rest))))
    return jnp.stack(padded, axis=1).reshape(w.shape[0], win, win)


def kernel(x_prompt, x_sample, state_conv, state_h, c_prompt, c_sample, w_ada, b_ada, g_norm1, g_norm2, g_final,
           w_in_a, b_in_a, g_v_a, b_v_a, w_s_a, b_s_a, w_out_a, w_in_b, b_in_b, w_conv_b, b_conv_b, w_ra_b, b_ra_b,
           w_ix_b, b_ix_b, lam_b, w_out_b, w_ff1, w_ff3, w_ff2):
    nb, t_p, d = x_prompt.shape
    ns, t_s, _ = x_sample.shape
    depth = w_ada.shape[0]
    d_a = w_out_a.shape[1]
    d_rnn = w_out_b.shape[1]
    group = d_a // N_GROUPS_A
    block = d_rnn // N_BLOCKS_B
    mp, ms = nb * t_p, ns * t_s
    assert ns == PAT and PAT % nb == 0 and t_p % CHUNK == 0 and t_s <= CHUNK and nb == 4

    x = _to_time_major(x_prompt, x_sample.transpose(1, 0, 2).reshape(ms, d))
    c_pat = jnp.concatenate([jnp.tile(c_prompt, (PAT // nb, 1)), c_sample])
    mod = _ada(c_pat, w_ada, b_ada)

    w_out_b = w_out_b.astype(BF16)

    blocks_per_group = LANES // math.gcd(block, LANES)
    win = (-(-block // LANES) + 1) * LANES
    halo_p = -(-(CONV_W - 1) * nb // SUBLANES) * SUBLANES

    v_new, conv_p, h_p, conv_s, h_s = [], [], [], [], []
    for i in range(depth):
        j = i // 2
        h = _rms_mod(x, g_norm1[i], mod, i, 0, 1, mp)
        if i % 2 == 0:
            u = _mm_bias_act(h, w_in_a, b_in_a, j, 0, d_a, "gelu", BF16)
            v = _mm_bias_act(h, w_in_a, b_in_a, j, d_a, d_a, "gelu", F32)
            vn = _layernorm(v, g_v_a[j], b_v_a[j], BF16, 0, mp)
            vn_s = _layernorm(v, g_v_a[j], b_v_a[j], F32, mp, ms)
            bias = jnp.repeat(b_s_a[j], nb, axis=1)[:, :, None]
            y = _gate_prompt(u, vn, w_s_a[j], bias, mp, group, nb)
            w_small = jnp.where(jnp.tril(jnp.ones((t_s, t_s), bool)), w_s_a[j][:, :t_s, :t_s], 0.0)
            y = _gate_sample(w_small.reshape(N_GROUPS_A, t_s * t_s), b_s_a[j][:, :t_s], u, vn_s, y, mp, t_s, ns, group)
            v_new.append(vn_s.reshape(t_s, ns, d_a).transpose(1, 0, 2))
            x = _mm_resid(y, w_out_a, j, x, mod, i, 2, mp, 1024, 512)
        else:
            gg = _mm_bias_act(h, w_in_b, b_in_b, j, 0, d_rnn, "gelu", BF16)
            xb = _mm_bias_act(h, w_in_b, b_in_b, j, d_rnn, d_rnn, None, F32)
            wra = _pad_block_weights(0.5 * w_ra_b[j], block, blocks_per_group, win)
            wix = _pad_block_weights(0.5 * w_ix_b[j], block, blocks_per_group, win)
            common = (w_conv_b[j], b_conv_b[j], wra, b_ra_b[j], wix, b_ix_b[j], lam_b[j])
            geom = dict(block=block, blocks_per_group=blocks_per_group, win=win)
            y, cp, hp = _rglru(xb, gg, jnp.zeros((halo_p, d_rnn), F32), jnp.zeros((SUBLANES, d_rnn), F32), *common,
                               None, row0=0, nrows=mp, p=nb, bt=512, **geom)
            hist_s = state_conv[j].transpose(1, 0, 2).reshape((CONV_W - 1) * ns, d_rnn)
            y, cs, hs = _rglru(xb, gg, hist_s, state_h[j], *common, y, row0=mp, nrows=ms, p=ns, bt=ns, **geom)
            conv_p.append(cp[halo_p - (CONV_W - 1) * nb:].reshape(CONV_W - 1, nb, d_rnn).transpose(1, 0, 2))
            h_p.append(hp[SUBLANES - nb:])
            conv_s.append(cs.reshape(CONV_W - 1, ns, d_rnn).transpose(1, 0, 2))
            h_s.append(hs)
            x = _mm_resid(y, w_out_b, j, x, mod, i, 2, mp, 1024, 512)
        h = _rms_mod(x, g_norm2[i], mod, i, 3, 4, mp)
        f, w2b = _mm_swiglu(h, w_ff1, w_ff3, w_ff2, i)
        x = _mm_resid(f, w2b[None], 0, x, mod, i, 5, mp, 512, 512)

    y_prompt = _rms_seq_major(x, g_final, nb, t_p)
    y_sample = _rms(x, g_final, mp, ms).reshape(t_s, ns, d).transpose(1, 0, 2)
    return (y_prompt, y_sample, jnp.stack(v_new), jnp.stack(conv_p), jnp.stack(h_p), jnp.stack(conv_s),
            jnp.stack(h_s))
```

```python
import functools
import math

import jax
import jax.numpy as jnp
from jax import lax
from jax.experimental import pallas as pl
from jax.experimental.pallas import tpu as pltpu

EPS = 1e-6
LRU_C = 8.0
CHUNK = 128
N_GROUPS_A = 16
N_BLOCKS_B = 16
CONV_W = 4

LANES = 128
SUBLANES = 8
PAT = 128
VMEM_LIMIT_BYTES = 56 * 1024 * 1024

F32 = jnp.float32
BF16 = jnp.bfloat16


def _params(*sem):
    return pltpu.CompilerParams(dimension_semantics=sem, vmem_limit_bytes=VMEM_LIMIT_BYTES)


def _sigmoid(x):
    return 0.5 * (jnp.tanh(0.5 * x) + 1.0)


def _ada_kernel(c_ref, w_ref, b_ref, o_ref, act_ref):
    @pl.when((pl.program_id(0) == 0) & (pl.program_id(1) == 0))
    def _():
        c = c_ref[...]
        act_ref[...] = (c * _sigmoid(c)).astype(BF16)

    w = w_ref[...].astype(BF16)
    o_ref[...] = jnp.dot(act_ref[...], w, preferred_element_type=F32) + b_ref[...]


def _ada(c_pat, w_ada, b_ada, bn=512):
    depth, d, n6 = w_ada.shape
    rows = c_pat.shape[0]
    return pl.pallas_call(
        _ada_kernel,
        grid=(depth, n6 // bn),
        in_specs=[
            pl.BlockSpec((rows, d), lambda i, n: (0, 0)),
            pl.BlockSpec((None, d, bn), lambda i, n: (i, 0, n)),
            pl.BlockSpec((None, 1, bn), lambda i, n: (i, 0, n)),
        ],
        out_specs=pl.BlockSpec((None, rows, bn), lambda i, n: (i, 0, n)),
        out_shape=jax.ShapeDtypeStruct((depth, rows, n6), F32),
        scratch_shapes=[pltpu.VMEM((rows, d), BF16)],
        compiler_params=_params("arbitrary", "arbitrary"),
        name="ada",
    )(c_pat, w_ada, b_ada.reshape(depth, 1, n6))


def _rms_mod_kernel(x_ref, g_ref, sh_ref, sc_ref, o_ref):
    g = g_ref[...]
    scale = 1.0 + sc_ref[...]
    shift = sh_ref[...]
    for k in range(x_ref.shape[0] // PAT):
        rows = slice(k * PAT, (k + 1) * PAT)
        x = x_ref[rows, :]
        y = x * lax.rsqrt(jnp.mean(x * x, axis=-1, keepdims=True) + EPS) * g
        o_ref[rows, :] = (y * scale + shift).astype(o_ref.dtype)


def _rms_mod(x, g, mod, layer, piece_shift, piece_scale, mp, bm=512):
    m, d = x.shape
    n_prompt_tiles = mp // bm

    def pat(piece):
        return lambda i: (layer, jnp.where(i >= n_prompt_tiles, 1, 0), piece)

    return pl.pallas_call(
        _rms_mod_kernel,
        grid=(m // bm,),
        in_specs=[
            pl.BlockSpec((bm, d), lambda i: (i, 0)),
            pl.BlockSpec((1, d), lambda i: (0, 0)),
            pl.BlockSpec((None, PAT, d), pat(piece_shift)),
            pl.BlockSpec((None, PAT, d), pat(piece_scale)),
        ],
        out_specs=pl.BlockSpec((bm, d), lambda i: (i, 0)),
        out_shape=jax.ShapeDtypeStruct((m, d), BF16),
        compiler_params=_params("arbitrary"),
        name="rms_mod",
    )(x, g.reshape(1, d), mod, mod)


def _rms_kernel(x_ref, g_ref, o_ref):
    x = x_ref[...]
    o_ref[...] = x * lax.rsqrt(jnp.mean(x * x, axis=-1, keepdims=True) + EPS) * g_ref[...]


def _rms(x, g, row0, nrows, bm=256):
    d = x.shape[1]
    off = row0 // bm
    return pl.pallas_call(
        _rms_kernel,
        grid=(nrows // bm,),
        in_specs=[pl.BlockSpec((bm, d), lambda i: (i + off, 0)), pl.BlockSpec((1, d), lambda i: (0, 0))],
        out_specs=pl.BlockSpec((bm, d), lambda i: (i, 0)),
        out_shape=jax.ShapeDtypeStruct((nrows, d), F32),
        compiler_params=_params("arbitrary"),
        name="rms_final",
    )(x, g.reshape(1, d))


def _rms_seq_major_kernel(x_ref, g_ref, o_ref, s_ref):
    nb, bt, d = o_ref.shape
    x = x_ref[...]
    y = x * lax.rsqrt(jnp.mean(x * x, axis=-1, keepdims=True) + EPS) * g_ref[...]
    for c in range(d // LANES):
        s_ref[c] = y[:, c * LANES:(c + 1) * LANES]
    for c in range(d // LANES):
        for s in range(nb):
            o_ref[s, :, c * LANES:(c + 1) * LANES] = s_ref[c, pl.ds(s, bt, stride=nb), :]


def _rms_seq_major(x, g, nb, t_len, bt=128):
    d = x.shape[1]
    return pl.pallas_call(
        _rms_seq_major_kernel,
        grid=(t_len // bt,),
        in_specs=[pl.BlockSpec((bt * nb, d), lambda i: (i, 0)), pl.BlockSpec((1, d), lambda i: (0, 0))],
        out_specs=pl.BlockSpec((nb, bt, d), lambda i: (0, i, 0)),
        out_shape=jax.ShapeDtypeStruct((nb, t_len, d), F32),
        scratch_shapes=[pltpu.VMEM((d // LANES, bt * nb, LANES), F32)],
        compiler_params=_params("arbitrary"),
        name="rms_final_prompt",
    )(x, g.reshape(1, d))


def _to_time_major_kernel(xp_ref, xs_ref, o_ref, s_ref, *, n_prompt_tiles):
    nb, bt, d = xp_ref.shape
    i = pl.program_id(0)

    @pl.when(i < n_prompt_tiles)
    def _():
        for c in range(d // LANES):
            for s in range(nb):
                s_ref[c, pl.ds(s, bt, stride=nb), :] = xp_ref[s, :, c * LANES:(c + 1) * LANES]
        for c in range(d // LANES):
            o_ref[:, c * LANES:(c + 1) * LANES] = s_ref[c]

    @pl.when(i >= n_prompt_tiles)
    def _():
        o_ref[...] = xs_ref[...]


def _to_time_major(x_prompt, xs_tm, bt=128):
    nb, t_len, d = x_prompt.shape
    ms = xs_tm.shape[0]
    rows = bt * nb
    n_prompt_tiles = t_len // bt
    return pl.pallas_call(
        functools.partial(_to_time_major_kernel, n_prompt_tiles=n_prompt_tiles),
        grid=(n_prompt_tiles + ms // rows,),
        in_specs=[
            pl.BlockSpec((nb, bt, d), lambda i: (0, jnp.minimum(i, n_prompt_tiles - 1), 0)),
            pl.BlockSpec((rows, d), lambda i: (jnp.maximum(i - n_prompt_tiles, 0), 0)),
        ],
        out_specs=pl.BlockSpec((rows, d), lambda i: (i, 0)),
        out_shape=jax.ShapeDtypeStruct((nb * t_len + ms, d), F32),
        scratch_shapes=[pltpu.VMEM((d // LANES, rows, LANES), F32)],
        compiler_params=_params("arbitrary"),
        name="to_time_major",
    )(x_prompt, xs_tm)


def _layernorm_kernel(x_ref, g_ref, b_ref, o_ref):
    x = x_ref[...]
    mu = jnp.mean(x, axis=-1, keepdims=True)
    xc = x - mu
    var = jnp.mean(xc * xc, axis=-1, keepdims=True)
    o_ref[...] = (xc * lax.rsqrt(var + EPS) * g_ref[...] + b_ref[...]).astype(o_ref.dtype)


def _layernorm(x, g, b, out_dtype, row0, nrows, bm=256):
    d = x.shape[1]
    off = row0 // bm
    return pl.pallas_call(
        _layernorm_kernel,
        grid=(nrows // bm,),
        in_specs=[
            pl.BlockSpec((bm, d), lambda i: (i + off, 0)),
            pl.BlockSpec((1, d), lambda i: (0, 0)),
            pl.BlockSpec((1, d), lambda i: (0, 0)),
        ],
        out_specs=pl.BlockSpec((bm, d), lambda i: (i, 0)),
        out_shape=jax.ShapeDtypeStruct((nrows, d), out_dtype),
        compiler_params=_params("arbitrary"),
        name="layernorm",
    )(x, g.reshape(1, d), b.reshape(1, d))


def _grid_order(order, n_i, n_j):
    if order == "mn":
        return (n_i, n_j), (lambda f: f)
    return (n_j, n_i), (lambda f: (lambda a, b: f(b, a)))


def _mm_bias_act_kernel(x_ref, w_ref, b_ref, o_ref, *, act):
    acc = jnp.dot(x_ref[...], w_ref[...].astype(BF16), preferred_element_type=F32) + b_ref[...]
    if act == "gelu":
        acc = jax.nn.gelu(acc)
    o_ref[...] = acc.astype(o_ref.dtype)


def _mm_bias_act(x, w, b, layer, col0, ncols, act, out_dtype, bm=1024, bn=512, order="mn"):
    m, k = x.shape
    noff = col0 // bn
    b3 = b.reshape(b.shape[0], 1, b.shape[1])
    grid, ix = _grid_order(order, m // bm, ncols // bn)
    return pl.pallas_call(
        functools.partial(_mm_bias_act_kernel, act=act),
        grid=grid,
        in_specs=[
            pl.BlockSpec((bm, k), ix(lambda i, n: (i, 0))),
            pl.BlockSpec((None, k, bn), ix(lambda i, n: (layer, 0, n + noff))),
            pl.BlockSpec((None, 1, bn), ix(lambda i, n: (layer, 0, n + noff))),
        ],
        out_specs=pl.BlockSpec((bm, bn), ix(lambda i, n: (i, n))),
        out_shape=jax.ShapeDtypeStruct((m, ncols), out_dtype),
        compiler_params=_params("arbitrary", "arbitrary"),
        name="mm_bias_act",
    )(x, w, b3)


def _mm_swiglu_kernel(x_ref, w1_ref, w3_ref, *rest, row_axis):
    x = x_ref[...]
    a = jnp.dot(x, w1_ref[...].astype(BF16), preferred_element_type=F32)
    b = jnp.dot(x, w3_ref[...].astype(BF16), preferred_element_type=F32)
    if len(rest) == 1:
        (o_ref,) = rest
    else:
        w2_ref, o_ref, w2b_ref = rest

        @pl.when(pl.program_id(row_axis) == 0)
        def _():
            w2b_ref[...] = w2_ref[...].astype(BF16)

    o_ref[...] = (a * _sigmoid(a) * b).astype(o_ref.dtype)


def _mm_swiglu(x, w1, w3, w2, layer, bm=1024, bn=256, order="mn"):
    m, k = x.shape
    n = w1.shape[2]
    nj = n // bn
    grid, ix = _grid_order(order, m // bm, nj)
    wspec = pl.BlockSpec((None, k, bn), ix(lambda i, j: (layer, 0, j)))
    in_specs = [pl.BlockSpec((bm, k), ix(lambda i, j: (i, 0))), wspec, wspec]
    out_specs = [pl.BlockSpec((bm, bn), ix(lambda i, j: (i, j)))]
    out_shape = [jax.ShapeDtypeStruct((m, n), BF16)]
    args = [x, w1, w3]
    if w2 is not None:
        rk, n2 = w2.shape[1] // nj, w2.shape[2]
        chunk = (lambda i, j: jnp.where(i == 0, j, nj - 1)) if order == "mn" else (lambda i, j: j)
        in_specs.append(pl.BlockSpec((None, rk, n2), ix(lambda i, j: (layer, chunk(i, j), 0))))
        out_specs.append(pl.BlockSpec((rk, n2), ix(lambda i, j: (chunk(i, j), 0))))
        out_shape.append(jax.ShapeDtypeStruct((nj * rk, n2), BF16))
        args.append(w2)
    return pl.pallas_call(
        functools.partial(_mm_swiglu_kernel, row_axis=0 if order == "mn" else 1),
        grid=grid,
        in_specs=in_specs,
        out_specs=out_specs,
        out_shape=out_shape,
        compiler_params=_params("arbitrary", "arbitrary"),
        name="mm_swiglu",
    )(*args)


def _mm_resid_kernel(x_ref, w_ref, res_ref, gate_ref, o_ref):
    acc = jnp.dot(x_ref[...], w_ref[...].astype(BF16), preferred_element_type=F32)
    gate = gate_ref[...]
    for k in range(acc.shape[0] // PAT):
        rows = slice(k * PAT, (k + 1) * PAT)
        o_ref[rows, :] = res_ref[rows, :] + gate * acc[rows, :]


def _mm_resid(x, w, layer_w, res, mod, layer, piece_gate, mp, bm, bn, order="mn"):
    m, k = x.shape
    n = w.shape[2]
    n_prompt_tiles = mp // bm
    gate_col0 = piece_gate * (n // bn)
    grid, ix = _grid_order(order, m // bm, n // bn)
    return pl.pallas_call(
        _mm_resid_kernel,
        grid=grid,
        in_specs=[
            pl.BlockSpec((bm, k), ix(lambda i, j: (i, 0))),
            pl.BlockSpec((None, k, bn), ix(lambda i, j: (layer_w, 0, j))),
            pl.BlockSpec((bm, bn), ix(lambda i, j: (i, j))),
            pl.BlockSpec((None, PAT, bn),
                         ix(lambda i, j: (layer, jnp.where(i >= n_prompt_tiles, 1, 0), gate_col0 + j))),
        ],
        out_specs=pl.BlockSpec((bm, bn), ix(lambda i, j: (i, j))),
        out_shape=jax.ShapeDtypeStruct((m, n), F32),
        input_output_aliases={2: 0},
        compiler_params=_params("arbitrary", "arbitrary"),
        name="mm_resid",
    )(x, w, res, mod)


def _gate_prompt_kernel(u_ref, v_ref, w_ref, bias_ref, y_ref, mix_ref, *, group, nb, n_chunks):
    c = pl.program_id(1)
    gpt, r, _ = mix_ref.shape
    chunk = w_ref.shape[1]

    @pl.when(c == 0)
    def _():
        shift = nb.bit_length() - 1
        rep = (lax.shift_right_logical(lax.broadcasted_iota(jnp.int32, (r, chunk), 0), shift)
               == lax.broadcasted_iota(jnp.int32, (r, chunk), 1)).astype(BF16)
        causal = lax.broadcasted_iota(jnp.int32, (chunk, chunk), 0) >= lax.broadcasted_iota(jnp.int32, (chunk, chunk), 1)
        same_seq = ((lax.broadcasted_iota(jnp.int32, (r, r), 0) & (nb - 1))
                    == (lax.broadcasted_iota(jnp.int32, (r, r), 1) & (nb - 1)))
        for g in range(gpt):
            wg = jnp.where(causal, w_ref[g], 0.0).astype(BF16)
            left = jnp.dot(rep, wg, preferred_element_type=F32).astype(BF16)
            full = lax.dot_general(left, rep, (((1,), (1,)), ((), ())), preferred_element_type=F32)
            mix_ref[g] = jnp.where(same_seq, full, 0.0).astype(BF16)

    @pl.when(c < n_chunks)
    def _():
        for g in range(gpt):
            cols = slice(g * group, (g + 1) * group)
            sv = jnp.dot(mix_ref[g], v_ref[:, cols], preferred_element_type=F32) + bias_ref[g]
            y_ref[:, cols] = (u_ref[:, cols].astype(F32) * sv).astype(y_ref.dtype)

    @pl.when(c >= n_chunks)
    def _():
        y_ref[...] = jnp.zeros(y_ref.shape, y_ref.dtype)


def _gate_prompt(u, vn, w_s, bias, mp, group, nb, col_tiles=2):
    m, d = u.shape
    n_groups, chunk, _ = w_s.shape
    r = chunk * nb
    gpt = n_groups // col_tiles
    bw = gpt * group
    n_chunks = mp // r
    return pl.pallas_call(
        functools.partial(_gate_prompt_kernel, group=group, nb=nb, n_chunks=n_chunks),
        grid=(col_tiles, m // r),
        in_specs=[
            pl.BlockSpec((r, bw), lambda h, c: (c, h)),
            pl.BlockSpec((r, bw), lambda h, c: (jnp.minimum(c, n_chunks - 1), h)),
            pl.BlockSpec((gpt, chunk, chunk), lambda h, c: (h, 0, 0)),
            pl.BlockSpec((gpt, r, 1), lambda h, c: (h, 0, 0)),
        ],
        out_specs=pl.BlockSpec((r, bw), lambda h, c: (c, h)),
        out_shape=jax.ShapeDtypeStruct((m, d), BF16),
        scratch_shapes=[pltpu.VMEM((gpt, r, r), BF16)],
        compiler_params=_params("arbitrary", "arbitrary"),
        name="gate_prompt",
    )(u, vn, w_s, bias)


def _gate_sample_kernel(w_ref, b_ref, u_ref, v_ref, y_any_ref, y_ref, *, steps, p):
    del y_any_ref
    g = pl.program_id(0)
    for t in range(steps):
        acc = w_ref[g, t * steps] * v_ref[0:p, :]
        for s in range(1, t + 1):
            acc = acc + w_ref[g, t * steps + s] * v_ref[s * p:(s + 1) * p, :]
        acc = acc + b_ref[g, t]
        rows = slice(t * p, (t + 1) * p)
        y_ref[rows, :] = (u_ref[rows, :].astype(F32) * acc).astype(y_ref.dtype)


def _gate_sample(w_small, b_small, u, vn_s, y, mp, steps, p, group):
    ms = steps * p
    n_groups = w_small.shape[0]
    row_blk = mp // ms
    return pl.pallas_call(
        functools.partial(_gate_sample_kernel, steps=steps, p=p),
        grid=(n_groups,),
        in_specs=[
            pl.BlockSpec(memory_space=pltpu.SMEM),
            pl.BlockSpec(memory_space=pltpu.SMEM),
            pl.BlockSpec((ms, group), lambda g: (row_blk, g)),
            pl.BlockSpec((ms, group), lambda g: (0, g)),
            pl.BlockSpec(memory_space=pl.ANY),
        ],
        out_specs=pl.BlockSpec((ms, group), lambda g: (row_blk, g)),
        out_shape=jax.ShapeDtypeStruct(y.shape, y.dtype),
        input_output_aliases={4: 0},
        compiler_params=_params("arbitrary"),
        name="gate_sample",
    )(w_small, b_small, u, vn_s, y)


def _window_starts(block, blocks_per_group, group_w, win):
    return [min((bl * block) // LANES * LANES, group_w - win) for bl in range(blocks_per_group)]


def _rglru_kernel(*refs, n_t, **static):
    y_ref = refs[-(_RGLRU_N_OUT + _RGLRU_N_SCRATCH)]
    t = pl.program_id(1)
    pl.when(t < n_t)(functools.partial(_rglru_tile, *refs, n_t=n_t, **static))

    @pl.when(t >= n_t)
    def _():
        y_ref[...] = jnp.zeros(y_ref.shape, y_ref.dtype)


_RGLRU_N_OUT = 3
_RGLRU_N_SCRATCH = 6


def _rglru_tile(*refs, p, halo, starts, win, aliased, n_t):
    if aliased:
        refs = refs[1:]
    (xb_ref, gg_ref, hist_ref, h0_ref, wc_ref, bc_ref, wra_ref, bra_ref, wix_ref, bix_ref, lam_ref,
     y_ref, conv_ref, hlast_ref, s_ref, xc_ref, xcb_ref, a_ref, bx_ref, hc_ref) = refs
    t = pl.program_id(1)
    bt, gw = xb_ref.shape

    @pl.when(t == 0)
    def _():
        s_ref[0:halo, :] = hist_ref[...]
        hc_ref[...] = h0_ref[...]

    s_ref[halo:halo + bt, :] = xb_ref[...]
    taps = [halo - (CONV_W - 1 - k) * p for k in range(CONV_W)]
    rc = min(bt, 64)

    def conv(i, carry):
        r0 = pl.multiple_of(i * rc, rc)
        sl = s_ref[pl.ds(r0, rc + halo), :]
        rl = sl if p % SUBLANES == 0 else pltpu.roll(sl, p, 0)
        xc = bc_ref[...]
        for k, lo in enumerate(taps):
            src = sl[lo:lo + rc] if lo % SUBLANES == 0 else rl[lo + p:lo + p + rc]
            xc = xc + wc_ref[k:k + 1, :] * src
        xc_ref[pl.ds(r0, rc), :] = xc
        xcb_ref[pl.ds(r0, rc), :] = xc.astype(BF16)
        return carry

    lax.fori_loop(0, bt // rc, conv, 0)
    s_ref[0:halo, :] = s_ref[bt:bt + halo, :]

    lam = lam_ref[...]
    c8h = (-0.5 * LRU_C) * (jnp.maximum(-lam, 0.0) + jnp.log1p(jnp.exp(-jnp.abs(lam))))

    rg = min(bt, 32)

    def gates(rows, cols):
        c = c8h[:, cols]
        log_a = c * jnp.tanh(a_ref[rows, cols]) + c
        a = jnp.exp(log_a)
        mult = jnp.sqrt(jnp.tanh(log_a) * (-1.0 - a * a))
        a_ref[rows, cols] = a
        bx_ref[rows, cols] = mult * (xc_ref[rows, cols] * (0.5 * jnp.tanh(bx_ref[rows, cols]) + 0.5))

    a_ref[...] = jnp.broadcast_to(0.5 * bra_ref[...], (bt, gw))
    bx_ref[...] = jnp.broadcast_to(0.5 * bix_ref[...], (bt, gw))
    col0 = 0
    for bl, lo in enumerate(starts):
        xw = xcb_ref[:, lo:lo + win]
        a_ref[:, lo:lo + win] += jnp.dot(xw, wra_ref[bl], preferred_element_type=F32)
        bx_ref[:, lo:lo + win] += jnp.dot(xw, wix_ref[bl], preferred_element_type=F32)
        col1 = starts[bl + 1] if bl + 1 < len(starts) else gw
        for r in range(bt // rg):
            gates(slice(r * rg, (r + 1) * rg), slice(col0, col1))
        col0 = col1

    if p % SUBLANES == 0:
        h = hc_ref[...]
        for step in range(bt // p):
            rows = slice(step * p, (step + 1) * p)
            h = a_ref[rows, :] * h + bx_ref[rows, :]
            y_ref[rows, :] = (gg_ref[rows, :].astype(F32) * h).astype(y_ref.dtype)
        hc_ref[...] = h
    else:
        upper = lax.broadcasted_iota(jnp.int32, (SUBLANES, gw), 0) >= p

        def scan(i, hc):
            rows = pl.ds(pl.multiple_of(i * SUBLANES, SUBLANES), SUBLANES)
            a8 = a_ref[rows, :]
            b8 = bx_ref[rows, :]
            h_lo = a8 * hc + b8
            h_hi = a8 * pltpu.roll(h_lo, p, 0) + b8
            a_ref[rows, :] = jnp.where(upper, h_hi, h_lo)
            return jnp.where(upper, h_hi, pltpu.roll(h_hi, p, 0))

        hc_ref[...] = lax.fori_loop(0, bt // SUBLANES, scan, hc_ref[...], unroll=2)

        def gate_out(i, carry):
            rows = pl.ds(pl.multiple_of(i * rg, rg), rg)
            y_ref[rows, :] = (gg_ref[rows, :].astype(F32) * a_ref[rows, :]).astype(y_ref.dtype)
            return carry

        lax.fori_loop(0, bt // rg, gate_out, 0)

    @pl.when(t == n_t - 1)
    def _():
        conv_ref[...] = s_ref[0:halo, :]
        hlast_ref[...] = hc_ref[...]


def _rglru(xb, gg, hist, h0, wconv, bconv, wra, bra, wix, bix, lam, y_prev, *, row0, nrows, p, bt,
           block, blocks_per_group, win):
    m, c = xb.shape
    gw = block * blocks_per_group
    n_groups = c // gw
    halo = hist.shape[0]
    hrows = h0.shape[0]
    starts = _window_starts(block, blocks_per_group, gw, win)
    rb = row0 // bt
    aliased = y_prev is not None
    n_t = nrows // bt
    n_zero = 0 if aliased else (m - row0 - nrows) // bt

    row_blk = lambda g, t: (rb + t, g)
    in_blk = lambda g, t: (rb + jnp.minimum(t, n_t - 1), g)
    vec = pl.BlockSpec((1, gw), lambda g, t: (0, g))
    wspec = pl.BlockSpec((blocks_per_group, win, win), lambda g, t: (g, 0, 0))
    in_specs = [
        pl.BlockSpec((bt, gw), in_blk),
        pl.BlockSpec((bt, gw), in_blk),
        pl.BlockSpec((halo, gw), lambda g, t: (0, g)),
        pl.BlockSpec((hrows, gw), lambda g, t: (0, g)),
        pl.BlockSpec((CONV_W, gw), lambda g, t: (0, g)),
        vec, wspec, vec, wspec, vec, vec,
    ]
    args = [xb, gg, hist, h0, wconv, bconv.reshape(1, c), wra, bra.reshape(1, c), wix, bix.reshape(1, c),
            lam.reshape(1, c)]
    aliases = {}
    if aliased:
        in_specs = [pl.BlockSpec(memory_space=pl.ANY)] + in_specs
        args = [y_prev] + args
        aliases = {0: 0}
    return pl.pallas_call(
        functools.partial(_rglru_kernel, p=p, halo=halo, starts=starts, win=win, aliased=aliased, n_t=n_t),
        grid=(n_groups, n_t + n_zero),
        in_specs=in_specs,
        out_specs=[
            pl.BlockSpec((bt, gw), row_blk),
            pl.BlockSpec((halo, gw), lambda g, t: (0, g)),
            pl.BlockSpec((hrows, gw), lambda g, t: (0, g)),
        ],
        out_shape=[
            jax.ShapeDtypeStruct((m, c), BF16),
            jax.ShapeDtypeStruct((halo, c), F32),
            jax.ShapeDtypeStruct((hrows, c), F32),
        ],
        scratch_shapes=[
            pltpu.VMEM((halo + bt, gw), F32),
            pltpu.VMEM((bt, gw), F32),
            pltpu.VMEM((bt, gw), BF16),
            pltpu.VMEM((bt, gw), F32),
            pltpu.VMEM((bt, gw), F32),
            pltpu.VMEM((hrows, gw), F32),
        ],
        input_output_aliases=aliases,
        compiler_params=_params("arbitrary", "arbitrary"),
        name="rglru",
    )(*args)


def _pad_block_weights(w, block, blocks_per_group, win):
    gw = block * blocks_per_group
    starts = _window_starts(block, blocks_per_group, gw, win)
    n_groups = w.shape[0] // blocks_per_group
    wg = w.astype(BF16).reshape(n_groups, blocks_per_group, block, block)
    padded = []
    for bl, lo in enumerate(starts):
        off = bl * block - lo
        rest = win - off - block
        padded.append(jnp.pad(wg[:, bl], ((0, 0), (off, rest), (off, rest))))
    return jnp.stack(padded, axis=1).reshape(w.shape[0], win, win)


def kernel(x_prompt, x_sample, state_conv, state_h, c_prompt, c_sample, w_ada, b_ada, g_norm1, g_norm2, g_final,
           w_in_a, b_in_a, g_v_a, b_v_a, w_s_a, b_s_a, w_out_a, w_in_b, b_in_b, w_conv_b, b_conv_b, w_ra_b, b_ra_b,
           w_ix_b, b_ix_b, lam_b, w_out_b, w_ff1, w_ff3, w_ff2):
    nb, t_p, d = x_prompt.shape
    ns, t_s, _ = x_sample.shape
    depth = w_ada.shape[0]
    d_a = w_out_a.shape[1]
    d_rnn = w_out_b.shape[1]
    group = d_a // N_GROUPS_A
    block = d_rnn // N_BLOCKS_B
    mp, ms = nb * t_p, ns * t_s
    assert ns == PAT and PAT % nb == 0 and t_p % CHUNK == 0 and t_s <= CHUNK and nb == 4

    x = _to_time_major(x_prompt, x_sample.transpose(1, 0, 2).reshape(ms, d))
    c_pat = jnp.concatenate([jnp.tile(c_prompt, (PAT // nb, 1)), c_sample])
    mod = _ada(c_pat, w_ada, b_ada)

    w_out_b = w_out_b.astype(BF16)

    blocks_per_group = LANES // math.gcd(block, LANES)
    win = (-(-block // LANES) + 1) * LANES
    halo_p = -(-(CONV_W - 1) * nb // SUBLANES) * SUBLANES

    v_new, conv_p, h_p, conv_s, h_s = [], [], [], [], []
    big = 1536 if (mp + ms) % 1536 == 0 else 1024
    ctl = dict(first={}, second={}, out={}, up={}, down={}, w2_outside=False)
    variants = [
        ctl,
        dict(ctl, up=dict(order="nm"), down=dict(order="nm")),
        dict(first=dict(order="nm"), second=dict(bm=big, bn=256), out=dict(order="nm"), up=dict(bm=big), down={},
             w2_outside=True),
        dict(ctl, first=dict(order="nm"), second=dict(bm=big, bn=256), out=dict(order="nm")),
    ]
    for i in range(depth):
        j = i // 2
        ab = variants[i % len(variants)]
        h = _rms_mod(x, g_norm1[i], mod, i, 0, 1, mp)
        if i % 2 == 0:
            u = _mm_bias_act(h, w_in_a, b_in_a, j, 0, d_a, "gelu", BF16, **ab["first"])
            v = _mm_bias_act(h, w_in_a, b_in_a, j, d_a, d_a, "gelu", F32, **ab["second"])
            vn = _layernorm(v, g_v_a[j], b_v_a[j], BF16, 0, mp)
            vn_s = _layernorm(v, g_v_a[j], b_v_a[j], F32, mp, ms)
            bias = jnp.repeat(b_s_a[j], nb, axis=1)[:, :, None]
            y = _gate_prompt(u, vn, w_s_a[j], bias, mp, group, nb)
            w_small = jnp.where(jnp.tril(jnp.ones((t_s, t_s), bool)), w_s_a[j][:, :t_s, :t_s], 0.0)
            y = _gate_sample(w_small.reshape(N_GROUPS_A, t_s * t_s), b_s_a[j][:, :t_s], u, vn_s, y, mp, t_s, ns, group)
            v_new.append(vn_s.reshape(t_s, ns, d_a).transpose(1, 0, 2))
            x = _mm_resid(y, w_out_a, j, x, mod, i, 2, mp, 1024, 512, **ab["out"])
        else:
            gg = _mm_bias_act(h, w_in_b, b_in_b, j, 0, d_rnn, "gelu", BF16, **ab["first"])
            xb = _mm_bias_act(h, w_in_b, b_in_b, j, d_rnn, d_rnn, None, F32, **ab["second"])
            wra = _pad_block_weights(0.5 * w_ra_b[j], block, blocks_per_group, win)
            wix = _pad_block_weights(0.5 * w_ix_b[j], block, blocks_per_group, win)
            common = (w_conv_b[j], b_conv_b[j], wra, b_ra_b[j], wix, b_ix_b[j], lam_b[j])
            geom = dict(block=block, blocks_per_group=blocks_per_group, win=win)
            y, cp, hp = _rglru(xb, gg, jnp.zeros((halo_p, d_rnn), F32), jnp.zeros((SUBLANES, d_rnn), F32), *common,
                               None, row0=0, nrows=mp, p=nb, bt=512, **geom)
            hist_s = state_conv[j].transpose(1, 0, 2).reshape((CONV_W - 1) * ns, d_rnn)
            y, cs, hs = _rglru(xb, gg, hist_s, state_h[j], *common, y, row0=mp, nrows=ms, p=ns, bt=ns, **geom)
            conv_p.append(cp[halo_p - (CONV_W - 1) * nb:].reshape(CONV_W - 1, nb, d_rnn).transpose(1, 0, 2))
            h_p.append(hp[SUBLANES - nb:])
            conv_s.append(cs.reshape(CONV_W - 1, ns, d_rnn).transpose(1, 0, 2))
            h_s.append(hs)
            x = _mm_resid(y, w_out_b, j, x, mod, i, 2, mp, 1024, 512, **ab["out"])
        h = _rms_mod(x, g_norm2[i], mod, i, 3, 4, mp)
        if ab["w2_outside"]:
            (f,) = _mm_swiglu(h, w_ff1, w_ff3, None, i, **ab["up"])
            w2b = w_ff2[i].astype(BF16)
        else:
            f, w2b = _mm_swiglu(h, w_ff1, w_ff3, w_ff2, i, **ab["up"])
        x = _mm_resid(f, w2b[None], 0, x, mod, i, 5, mp, 512, 512, **ab["down"])

    y_prompt = _rms_seq_major(x, g_final, nb, t_p)
    y_sample = _rms(x, g_final, mp, ms).reshape(t_s, ns, d).transpose(1, 0, 2)
    return (y_prompt, y_sample, jnp.stack(v_new), jnp.stack(conv_p), jnp.stack(h_p), jnp.stack(conv_s),
            jnp.stack(h_s))
```

```python
import functools
import math

import jax
import jax.numpy as jnp
from jax import lax
from jax.experimental import pallas as pl
from jax.experimental.pallas import tpu as pltpu

EPS = 1e-6
LRU_C = 8.0
CHUNK = 128
N_GROUPS_A = 16
N_BLOCKS_B = 16
CONV_W = 4

LANES = 128
SUBLANES = 8
PAT = 128
VMEM_LIMIT_BYTES = 56 * 1024 * 1024

F32 = jnp.float32
BF16 = jnp.bfloat16


def _params(*sem):
    return pltpu.CompilerParams(dimension_semantics=sem, vmem_limit_bytes=VMEM_LIMIT_BYTES)


def _sigmoid(x):
    return 0.5 * (jnp.tanh(0.5 * x) + 1.0)


def _ada_kernel(c_ref, w_ref, b_ref, o_ref, act_ref):
    @pl.when((pl.program_id(0) == 0) & (pl.program_id(1) == 0))
    def _():
        c = c_ref[...]
        act_ref[...] = (c * _sigmoid(c)).astype(BF16)

    w = w_ref[...].astype(BF16)
    o_ref[...] = jnp.dot(act_ref[...], w, preferred_element_type=F32) + b_ref[...]


def _ada(c_pat, w_ada, b_ada, bn=1024):
    depth, d, n6 = w_ada.shape
    rows = c_pat.shape[0]
    return pl.pallas_call(
        _ada_kernel,
        grid=(depth, n6 // bn),
        in_specs=[
            pl.BlockSpec((rows, d), lambda i, n: (0, 0)),
            pl.BlockSpec((None, d, bn), lambda i, n: (i, 0, n)),
            pl.BlockSpec((None, 1, bn), lambda i, n: (i, 0, n)),
        ],
        out_specs=pl.BlockSpec((None, rows, bn), lambda i, n: (i, 0, n)),
        out_shape=jax.ShapeDtypeStruct((depth, rows, n6), F32),
        scratch_shapes=[pltpu.VMEM((rows, d), BF16)],
        compiler_params=_params("arbitrary", "arbitrary"),
        name="ada",
    )(c_pat, w_ada, b_ada.reshape(depth, 1, n6))


def _rms_mod_kernel(x_ref, g_ref, sh_ref, sc_ref, o_ref):
    g = g_ref[...]
    scale = 1.0 + sc_ref[...]
    shift = sh_ref[...]
    for k in range(x_ref.shape[0] // PAT):
        rows = slice(k * PAT, (k + 1) * PAT)
        x = x_ref[rows, :]
        y = x * lax.rsqrt(jnp.mean(x * x, axis=-1, keepdims=True) + EPS) * g
        o_ref[rows, :] = (y * scale + shift).astype(o_ref.dtype)


def _rms_mod(x, g, mod, layer, piece_shift, piece_scale, mp, bm=512):
    m, d = x.shape
    n_prompt_tiles = mp // bm

    def pat(piece):
        return lambda i: (layer, jnp.where(i >= n_prompt_tiles, 1, 0), piece)

    return pl.pallas_call(
        _rms_mod_kernel,
        grid=(m // bm,),
        in_specs=[
            pl.BlockSpec((bm, d), lambda i: (i, 0)),
            pl.BlockSpec((1, d), lambda i: (0, 0)),
            pl.BlockSpec((None, PAT, d), pat(piece_shift)),
            pl.BlockSpec((None, PAT, d), pat(piece_scale)),
        ],
        out_specs=pl.BlockSpec((bm, d), lambda i: (i, 0)),
        out_shape=jax.ShapeDtypeStruct((m, d), BF16),
        compiler_params=_params("arbitrary"),
        name="rms_mod",
    )(x, g.reshape(1, d), mod, mod)


def _rms_kernel(x_ref, g_ref, o_ref):
    x = x_ref[...]
    o_ref[...] = x * lax.rsqrt(jnp.mean(x * x, axis=-1, keepdims=True) + EPS) * g_ref[...]


def _rms(x, g, row0, nrows, bm=256):
    d = x.shape[1]
    off = row0 // bm
    return pl.pallas_call(
        _rms_kernel,
        grid=(nrows // bm,),
        in_specs=[pl.BlockSpec((bm, d), lambda i: (i + off, 0)), pl.BlockSpec((1, d), lambda i: (0, 0))],
        out_specs=pl.BlockSpec((bm, d), lambda i: (i, 0)),
        out_shape=jax.ShapeDtypeStruct((nrows, d), F32),
        compiler_params=_params("arbitrary"),
        name="rms_final",
    )(x, g.reshape(1, d))


def _rms_seq_major_kernel(x_ref, g_ref, o_ref, s_ref):
    nb, bt, d = o_ref.shape
    x = x_ref[...]
    y = x * lax.rsqrt(jnp.mean(x * x, axis=-1, keepdims=True) + EPS) * g_ref[...]
    for c in range(d // LANES):
        s_ref[c] = y[:, c * LANES:(c + 1) * LANES]
    for c in range(d // LANES):
        for s in range(nb):
            o_ref[s, :, c * LANES:(c + 1) * LANES] = s_ref[c, pl.ds(s, bt, stride=nb), :]


def _rms_seq_major(x, g, nb, t_len, bt=128):
    d = x.shape[1]
    return pl.pallas_call(
        _rms_seq_major_kernel,
        grid=(t_len // bt,),
        in_specs=[pl.BlockSpec((bt * nb, d), lambda i: (i, 0)), pl.BlockSpec((1, d), lambda i: (0, 0))],
        out_specs=pl.BlockSpec((nb, bt, d), lambda i: (0, i, 0)),
        out_shape=jax.ShapeDtypeStruct((nb, t_len, d), F32),
        scratch_shapes=[pltpu.VMEM((d // LANES, bt * nb, LANES), F32)],
        compiler_params=_params("arbitrary"),
        name="rms_final_prompt",
    )(x, g.reshape(1, d))


def _to_time_major_kernel(xp_ref, xs_ref, o_ref, s_ref, *, n_prompt_tiles):
    nb, bt, d = xp_ref.shape
    i = pl.program_id(0)

    @pl.when(i < n_prompt_tiles)
    def _():
        for c in range(d // LANES):
            for s in range(nb):
                s_ref[c, pl.ds(s, bt, stride=nb), :] = xp_ref[s, :, c * LANES:(c + 1) * LANES]
        for c in range(d // LANES):
            o_ref[:, c * LANES:(c + 1) * LANES] = s_ref[c]

    @pl.when(i >= n_prompt_tiles)
    def _():
        o_ref[...] = xs_ref[...]


def _to_time_major(x_prompt, xs_tm, bt=128):
    nb, t_len, d = x_prompt.shape
    ms = xs_tm.shape[0]
    rows = bt * nb
    n_prompt_tiles = t_len // bt
    return pl.pallas_call(
        functools.partial(_to_time_major_kernel, n_prompt_tiles=n_prompt_tiles),
        grid=(n_prompt_tiles + ms // rows,),
        in_specs=[
            pl.BlockSpec((nb, bt, d), lambda i: (0, jnp.minimum(i, n_prompt_tiles - 1), 0)),
            pl.BlockSpec((rows, d), lambda i: (jnp.maximum(i - n_prompt_tiles, 0), 0)),
        ],
        out_specs=pl.BlockSpec((rows, d), lambda i: (i, 0)),
        out_shape=jax.ShapeDtypeStruct((nb * t_len + ms, d), F32),
        scratch_shapes=[pltpu.VMEM((d // LANES, rows, LANES), F32)],
        compiler_params=_params("arbitrary"),
        name="to_time_major",
    )(x_prompt, xs_tm)


def _layernorm_kernel(x_ref, g_ref, b_ref, o_ref):
    x = x_ref[...]
    mu = jnp.mean(x, axis=-1, keepdims=True)
    xc = x - mu
    var = jnp.mean(xc * xc, axis=-1, keepdims=True)
    o_ref[...] = (xc * lax.rsqrt(var + EPS) * g_ref[...] + b_ref[...]).astype(o_ref.dtype)


def _layernorm(x, g, b, out_dtype, row0, nrows, bm=256):
    d = x.shape[1]
    off = row0 // bm
    return pl.pallas_call(
        _layernorm_kernel,
        grid=(nrows // bm,),
        in_specs=[
            pl.BlockSpec((bm, d), lambda i: (i + off, 0)),
            pl.BlockSpec((1, d), lambda i: (0, 0)),
            pl.BlockSpec((1, d), lambda i: (0, 0)),
        ],
        out_specs=pl.BlockSpec((bm, d), lambda i: (i, 0)),
        out_shape=jax.ShapeDtypeStruct((nrows, d), out_dtype),
        compiler_params=_params("arbitrary"),
        name="layernorm",
    )(x, g.reshape(1, d), b.reshape(1, d))


def _grid_order(order, n_i, n_j):
    if order == "mn":
        return (n_i, n_j), (lambda f: f)
    return (n_j, n_i), (lambda f: (lambda a, b: f(b, a)))


def _mm_bias_act_kernel(x_ref, w_ref, b_ref, o_ref, *, act):
    acc = jnp.dot(x_ref[...], w_ref[...].astype(BF16), preferred_element_type=F32) + b_ref[...]
    if act == "gelu":
        acc = jax.nn.gelu(acc)
    o_ref[...] = acc.astype(o_ref.dtype)


def _mm_bias_act(x, w, b, layer, col0, ncols, act, out_dtype, bm=1024, bn=512, order="mn"):
    m, k = x.shape
    noff = col0 // bn
    b3 = b.reshape(b.shape[0], 1, b.shape[1])
    grid, ix = _grid_order(order, m // bm, ncols // bn)
    return pl.pallas_call(
        functools.partial(_mm_bias_act_kernel, act=act),
        grid=grid,
        in_specs=[
            pl.BlockSpec((bm, k), ix(lambda i, n: (i, 0))),
            pl.BlockSpec((None, k, bn), ix(lambda i, n: (layer, 0, n + noff))),
            pl.BlockSpec((None, 1, bn), ix(lambda i, n: (layer, 0, n + noff))),
        ],
        out_specs=pl.BlockSpec((bm, bn), ix(lambda i, n: (i, n))),
        out_shape=jax.ShapeDtypeStruct((m, ncols), out_dtype),
        compiler_params=_params("arbitrary", "arbitrary"),
        name="mm_bias_act",
    )(x, w, b3)


def _mm_swiglu_kernel(x_ref, w1_ref, w3_ref, w2_ref, o_ref, w2b_ref):
    x = x_ref[...]
    a = jnp.dot(x, w1_ref[...].astype(BF16), preferred_element_type=F32)
    b = jnp.dot(x, w3_ref[...].astype(BF16), preferred_element_type=F32)
    o_ref[...] = (a * _sigmoid(a) * b).astype(o_ref.dtype)

    @pl.when(pl.program_id(0) == 0)
    def _():
        w2b_ref[...] = w2_ref[...].astype(BF16)


def _mm_swiglu(x, w1, w3, w2, layer, bm=1024, bn=256):
    m, k = x.shape
    n = w1.shape[2]
    nj = n // bn
    rk, n2 = w2.shape[1] // nj, w2.shape[2]
    wspec = pl.BlockSpec((None, k, bn), lambda i, j: (layer, 0, j))
    chunk = lambda i, j: jnp.where(i == 0, j, nj - 1)
    return pl.pallas_call(
        _mm_swiglu_kernel,
        grid=(m // bm, nj),
        in_specs=[
            pl.BlockSpec((bm, k), lambda i, j: (i, 0)), wspec, wspec,
            pl.BlockSpec((None, rk, n2), lambda i, j: (layer, chunk(i, j), 0)),
        ],
        out_specs=[
            pl.BlockSpec((bm, bn), lambda i, j: (i, j)),
            pl.BlockSpec((rk, n2), lambda i, j: (chunk(i, j), 0)),
        ],
        out_shape=[jax.ShapeDtypeStruct((m, n), BF16), jax.ShapeDtypeStruct((nj * rk, n2), BF16)],
        compiler_params=_params("arbitrary", "arbitrary"),
        name="mm_swiglu",
    )(x, w1, w3, w2)


def _mm_resid_kernel(x_ref, w_ref, res_ref, gate_ref, o_ref):
    acc = jnp.dot(x_ref[...], w_ref[...].astype(BF16), preferred_element_type=F32)
    gate = gate_ref[...]
    for k in range(acc.shape[0] // PAT):
        rows = slice(k * PAT, (k + 1) * PAT)
        o_ref[rows, :] = res_ref[rows, :] + gate * acc[rows, :]


def _mm_resid(x, w, layer_w, res, mod, layer, piece_gate, mp, bm, bn, order="mn"):
    m, k = x.shape
    n = w.shape[2]
    n_prompt_tiles = mp // bm
    gate_col0 = piece_gate * (n // bn)
    grid, ix = _grid_order(order, m // bm, n // bn)
    return pl.pallas_call(
        _mm_resid_kernel,
        grid=grid,
        in_specs=[
            pl.BlockSpec((bm, k), ix(lambda i, j: (i, 0))),
            pl.BlockSpec((None, k, bn), ix(lambda i, j: (layer_w, 0, j))),
            pl.BlockSpec((bm, bn), ix(lambda i, j: (i, j))),
            pl.BlockSpec((None, PAT, bn),
                         ix(lambda i, j: (layer, jnp.where(i >= n_prompt_tiles, 1, 0), gate_col0 + j))),
        ],
        out_specs=pl.BlockSpec((bm, bn), ix(lambda i, j: (i, j))),
        out_shape=jax.ShapeDtypeStruct((m, n), F32),
        input_output_aliases={2: 0},
        compiler_params=_params("arbitrary", "arbitrary"),
        name="mm_resid",
    )(x, w, res, mod)


def _gate_prompt_kernel(u_ref, v_ref, gv_ref, bv_ref, w_ref, bias_ref, y_ref, mix_ref, vn_ref, *, group, nb, n_chunks):
    c = pl.program_id(0)
    gpt, r, _ = mix_ref.shape
    chunk = w_ref.shape[1]

    @pl.when(c == 0)
    def _():
        shift = nb.bit_length() - 1
        rep = (lax.shift_right_logical(lax.broadcasted_iota(jnp.int32, (r, chunk), 0), shift)
               == lax.broadcasted_iota(jnp.int32, (r, chunk), 1)).astype(BF16)
        causal = lax.broadcasted_iota(jnp.int32, (chunk, chunk), 0) >= lax.broadcasted_iota(jnp.int32, (chunk, chunk), 1)
        same_seq = ((lax.broadcasted_iota(jnp.int32, (r, r), 0) & (nb - 1))
                    == (lax.broadcasted_iota(jnp.int32, (r, r), 1) & (nb - 1)))
        for g in range(gpt):
            wg = jnp.where(causal, w_ref[g], 0.0).astype(BF16)
            left = jnp.dot(rep, wg, preferred_element_type=F32).astype(BF16)
            full = lax.dot_general(left, rep, (((1,), (1,)), ((), ())), preferred_element_type=F32)
            mix_ref[g] = jnp.where(same_seq, full, 0.0).astype(BF16)

    @pl.when(c < n_chunks)
    def _():
        gain = gv_ref[...]
        shift = bv_ref[...]
        for k in range(r // PAT):
            rows = slice(k * PAT, (k + 1) * PAT)
            x = v_ref[rows, :]
            xc = x - jnp.mean(x, axis=-1, keepdims=True)
            var = jnp.mean(xc * xc, axis=-1, keepdims=True)
            vn_ref[rows, :] = (xc * lax.rsqrt(var + EPS) * gain + shift).astype(BF16)
        for g in range(gpt):
            cols = slice(g * group, (g + 1) * group)
            sv = jnp.dot(mix_ref[g], vn_ref[:, cols], preferred_element_type=F32) + bias_ref[:, g:g + 1]
            y_ref[:, cols] = (u_ref[:, cols].astype(F32) * sv).astype(y_ref.dtype)

    @pl.when(c >= n_chunks)
    def _():
        y_ref[...] = jnp.zeros(y_ref.shape, y_ref.dtype)


def _gate_prompt(u, v, g_v, b_v, w_s, bias_t, mp, group, nb):
    m, d = u.shape
    n_groups, chunk, _ = w_s.shape
    r = chunk * nb
    n_chunks = mp // r
    row_blk = lambda c: (jnp.minimum(c, n_chunks - 1), 0)
    whole = lambda shape: pl.BlockSpec(shape, lambda c: (0,) * len(shape))
    return pl.pallas_call(
        functools.partial(_gate_prompt_kernel, group=group, nb=nb, n_chunks=n_chunks),
        grid=(m // r,),
        in_specs=[
            pl.BlockSpec((r, d), row_blk),
            pl.BlockSpec((r, d), row_blk),
            whole((1, d)), whole((1, d)), whole((n_groups, chunk, chunk)), whole((r, n_groups)),
        ],
        out_specs=pl.BlockSpec((r, d), lambda c: (c, 0)),
        out_shape=jax.ShapeDtypeStruct((m, d), BF16),
        scratch_shapes=[pltpu.VMEM((n_groups, r, r), BF16), pltpu.VMEM((r, d), BF16)],
        compiler_params=_params("arbitrary"),
        name="gate_prompt",
    )(u, v, g_v.reshape(1, d), b_v.reshape(1, d), w_s, bias_t)


def _gate_sample_kernel(w_ref, b_ref, u_ref, v_ref, y_any_ref, y_ref, *, steps, p):
    del y_any_ref
    g = pl.program_id(0)
    for t in range(steps):
        acc = w_ref[g, t * steps] * v_ref[0:p, :]
        for s in range(1, t + 1):
            acc = acc + w_ref[g, t * steps + s] * v_ref[s * p:(s + 1) * p, :]
        acc = acc + b_ref[g, t]
        rows = slice(t * p, (t + 1) * p)
        y_ref[rows, :] = (u_ref[rows, :].astype(F32) * acc).astype(y_ref.dtype)


def _gate_sample(w_small, b_small, u, vn_s, y, mp, steps, p, group):
    ms = steps * p
    n_groups = w_small.shape[0]
    row_blk = mp // ms
    return pl.pallas_call(
        functools.partial(_gate_sample_kernel, steps=steps, p=p),
        grid=(n_groups,),
        in_specs=[
            pl.BlockSpec(memory_space=pltpu.SMEM),
            pl.BlockSpec(memory_space=pltpu.SMEM),
            pl.BlockSpec((ms, group), lambda g: (row_blk, g)),
            pl.BlockSpec((ms, group), lambda g: (0, g)),
            pl.BlockSpec(memory_space=pl.ANY),
        ],
        out_specs=pl.BlockSpec((ms, group), lambda g: (row_blk, g)),
        out_shape=jax.ShapeDtypeStruct(y.shape, y.dtype),
        input_output_aliases={4: 0},
        compiler_params=_params("arbitrary"),
        name="gate_sample",
    )(w_small, b_small, u, vn_s, y)


def _window_starts(block, blocks_per_group, group_w, win):
    return [min((bl * block) // LANES * LANES, group_w - win) for bl in range(blocks_per_group)]


def _rglru_kernel(*refs, n_t, **static):
    y_ref = refs[-(_RGLRU_N_OUT + _RGLRU_N_SCRATCH)]
    t = pl.program_id(1)
    pl.when(t < n_t)(functools.partial(_rglru_tile, *refs, n_t=n_t, **static))

    @pl.when(t >= n_t)
    def _():
        y_ref[...] = jnp.zeros(y_ref.shape, y_ref.dtype)


_RGLRU_N_OUT = 3
_RGLRU_N_SCRATCH = 6


def _rglru_tile(*refs, p, halo, starts, win, aliased, n_t):
    if aliased:
        refs = refs[1:]
    (xb_ref, gg_ref, hist_ref, h0_ref, wc_ref, bc_ref, wra_ref, bra_ref, wix_ref, bix_ref, lam_ref,
     y_ref, conv_ref, hlast_ref, s_ref, xc_ref, xcb_ref, a_ref, bx_ref, hc_ref) = refs
    t = pl.program_id(1)
    bt, gw = xb_ref.shape

    @pl.when(t == 0)
    def _():
        s_ref[0:halo, :] = hist_ref[...]
        hc_ref[...] = h0_ref[...]

    s_ref[halo:halo + bt, :] = xb_ref[...]
    taps = [halo - (CONV_W - 1 - k) * p for k in range(CONV_W)]
    rc = min(bt, 64)

    def conv(i, carry):
        r0 = pl.multiple_of(i * rc, rc)
        sl = s_ref[pl.ds(r0, rc + halo), :]
        rl = sl if p % SUBLANES == 0 else pltpu.roll(sl, p, 0)
        xc = bc_ref[...]
        for k, lo in enumerate(taps):
            src = sl[lo:lo + rc] if lo % SUBLANES == 0 else rl[lo + p:lo + p + rc]
            xc = xc + wc_ref[k:k + 1, :] * src
        xc_ref[pl.ds(r0, rc), :] = xc
        xcb_ref[pl.ds(r0, rc), :] = xc.astype(BF16)
        return carry

    lax.fori_loop(0, bt // rc, conv, 0)
    s_ref[0:halo, :] = s_ref[bt:bt + halo, :]

    lam = lam_ref[...]
    c8h = (-0.5 * LRU_C) * (jnp.maximum(-lam, 0.0) + jnp.log1p(jnp.exp(-jnp.abs(lam))))

    rg = min(bt, 32)

    def gates(rows, cols):
        c = c8h[:, cols]
        log_a = c * jnp.tanh(a_ref[rows, cols]) + c
        a = jnp.exp(log_a)
        mult = jnp.sqrt(jnp.tanh(log_a) * (-1.0 - a * a))
        a_ref[rows, cols] = a
        bx_ref[rows, cols] = mult * (xc_ref[rows, cols] * (0.5 * jnp.tanh(bx_ref[rows, cols]) + 0.5))

    a_ref[...] = jnp.broadcast_to(0.5 * bra_ref[...], (bt, gw))
    bx_ref[...] = jnp.broadcast_to(0.5 * bix_ref[...], (bt, gw))
    col0 = 0
    for bl, lo in enumerate(starts):
        xw = xcb_ref[:, lo:lo + win]
        a_ref[:, lo:lo + win] += jnp.dot(xw, wra_ref[bl], preferred_element_type=F32)
        bx_ref[:, lo:lo + win] += jnp.dot(xw, wix_ref[bl], preferred_element_type=F32)
        col1 = starts[bl + 1] if bl + 1 < len(starts) else gw
        for r in range(bt // rg):
            gates(slice(r * rg, (r + 1) * rg), slice(col0, col1))
        col0 = col1

    if p % SUBLANES == 0:
        h = hc_ref[...]
        for step in range(bt // p):
            rows = slice(step * p, (step + 1) * p)
            h = a_ref[rows, :] * h + bx_ref[rows, :]
            y_ref[rows, :] = (gg_ref[rows, :].astype(F32) * h).astype(y_ref.dtype)
        hc_ref[...] = h
    else:
        upper = lax.broadcasted_iota(jnp.int32, (SUBLANES, gw), 0) >= p

        def scan(i, hc):
            rows = pl.ds(pl.multiple_of(i * SUBLANES, SUBLANES), SUBLANES)
            a8 = a_ref[rows, :]
            b8 = bx_ref[rows, :]
            h_lo = a8 * hc + b8
            h_hi = a8 * pltpu.roll(h_lo, p, 0) + b8
            a_ref[rows, :] = jnp.where(upper, h_hi, h_lo)
            return jnp.where(upper, h_hi, pltpu.roll(h_hi, p, 0))

        hc_ref[...] = lax.fori_loop(0, bt // SUBLANES, scan, hc_ref[...], unroll=2)

        def gate_out(i, carry):
            rows = pl.ds(pl.multiple_of(i * rg, rg), rg)
            y_ref[rows, :] = (gg_ref[rows, :].astype(F32) * a_ref[rows, :]).astype(y_ref.dtype)
            return carry

        lax.fori_loop(0, bt // rg, gate_out, 0)

    @pl.when(t == n_t - 1)
    def _():
        conv_ref[...] = s_ref[0:halo, :]
        hlast_ref[...] = hc_ref[...]


def _rglru(xb, gg, hist, h0, wconv, bconv, wra, bra, wix, bix, lam, y_prev, *, row0, nrows, p, bt,
           block, blocks_per_group, win):
    m, c = xb.shape
    gw = block * blocks_per_group
    n_groups = c // gw
    halo = hist.shape[0]
    hrows = h0.shape[0]
    starts = _window_starts(block, blocks_per_group, gw, win)
    rb = row0 // bt
    aliased = y_prev is not None
    n_t = nrows // bt
    n_zero = 0 if aliased else (m - row0 - nrows) // bt

    row_blk = lambda g, t: (rb + t, g)
    in_blk = lambda g, t: (rb + jnp.minimum(t, n_t - 1), g)
    vec = pl.BlockSpec((1, gw), lambda g, t: (0, g))
    wspec = pl.BlockSpec((blocks_per_group, win, win), lambda g, t: (g, 0, 0))
    in_specs = [
        pl.BlockSpec((bt, gw), in_blk),
        pl.BlockSpec((bt, gw), in_blk),
        pl.BlockSpec((halo, gw), lambda g, t: (0, g)),
        pl.BlockSpec((hrows, gw), lambda g, t: (0, g)),
        pl.BlockSpec((CONV_W, gw), lambda g, t: (0, g)),
        vec, wspec, vec, wspec, vec, vec,
    ]
    args = [xb, gg, hist, h0, wconv, bconv.reshape(1, c), wra, bra.reshape(1, c), wix, bix.reshape(1, c),
            lam.reshape(1, c)]
    aliases = {}
    if aliased:
        in_specs = [pl.BlockSpec(memory_space=pl.ANY)] + in_specs
        args = [y_prev] + args
        aliases = {0: 0}
    return pl.pallas_call(
        functools.partial(_rglru_kernel, p=p, halo=halo, starts=starts, win=win, aliased=aliased, n_t=n_t),
        grid=(n_groups, n_t + n_zero),
        in_specs=in_specs,
        out_specs=[
            pl.BlockSpec((bt, gw), row_blk),
            pl.BlockSpec((halo, gw), lambda g, t: (0, g)),
            pl.BlockSpec((hrows, gw), lambda g, t: (0, g)),
        ],
        out_shape=[
            jax.ShapeDtypeStruct((m, c), BF16),
            jax.ShapeDtypeStruct((halo, c), F32),
            jax.ShapeDtypeStruct((hrows, c), F32),
        ],
        scratch_shapes=[
            pltpu.VMEM((halo + bt, gw), F32),
            pltpu.VMEM((bt, gw), F32),
            pltpu.VMEM((bt, gw), BF16),
            pltpu.VMEM((bt, gw), F32),
            pltpu.VMEM((bt, gw), F32),
            pltpu.VMEM((hrows, gw), F32),
        ],
        input_output_aliases=aliases,
        compiler_params=_params("arbitrary", "arbitrary"),
        name="rglru",
    )(*args)


def _pad_block_weights(w, block, blocks_per_group, win):
    gw = block * blocks_per_group
    starts = _window_starts(block, blocks_per_group, gw, win)
    n_groups = w.shape[0] // blocks_per_group
    wg = w.astype(BF16).reshape(n_groups, blocks_per_group, block, block)
    padded = []
    for bl, lo in enumerate(starts):
        off = bl * block - lo
        rest = win - off - block
        padded.append(jnp.pad(wg[:, bl], ((0, 0), (off, rest), (off, rest))))
    return jnp.stack(padded, axis=1).reshape(w.shape[0], win, win)


def kernel(x_prompt, x_sample, state_conv, state_h, c_prompt, c_sample, w_ada, b_ada, g_norm1, g_norm2, g_final,
           w_in_a, b_in_a, g_v_a, b_v_a, w_s_a, b_s_a, w_out_a, w_in_b, b_in_b, w_conv_b, b_conv_b, w_ra_b, b_ra_b,
           w_ix_b, b_ix_b, lam_b, w_out_b, w_ff1, w_ff3, w_ff2):
    nb, t_p, d = x_prompt.shape
    ns, t_s, _ = x_sample.shape
    depth = w_ada.shape[0]
    d_a = w_out_a.shape[1]
    d_rnn = w_out_b.shape[1]
    group = d_a // N_GROUPS_A
    block = d_rnn // N_BLOCKS_B
    mp, ms = nb * t_p, ns * t_s
    assert ns == PAT and PAT % nb == 0 and t_p % CHUNK == 0 and t_s <= CHUNK and nb == 4

    x = _to_time_major(x_prompt, x_sample.transpose(1, 0, 2).reshape(ms, d))
    c_pat = jnp.concatenate([jnp.tile(c_prompt, (PAT // nb, 1)), c_sample])
    mod = _ada(c_pat, w_ada, b_ada)

    w_out_b = w_out_b.astype(BF16)

    blocks_per_group = LANES // math.gcd(block, LANES)
    win = (-(-block // LANES) + 1) * LANES
    halo_p = -(-(CONV_W - 1) * nb // SUBLANES) * SUBLANES

    v_new, conv_p, h_p, conv_s, h_s = [], [], [], [], []
    for i in range(depth):
        j = i // 2
        h = _rms_mod(x, g_norm1[i], mod, i, 0, 1, mp)
        if i % 2 == 0:
            u = _mm_bias_act(h, w_in_a, b_in_a, j, 0, d_a, "gelu", BF16)
            v = _mm_bias_act(h, w_in_a, b_in_a, j, d_a, d_a, "gelu", F32)
            vn_s = _layernorm(v, g_v_a[j], b_v_a[j], F32, mp, ms)
            bias_t = jnp.repeat(b_s_a[j], nb, axis=1).T
            y = _gate_prompt(u, v, g_v_a[j], b_v_a[j], w_s_a[j], bias_t, mp, group, nb)
            w_small = jnp.where(jnp.tril(jnp.ones((t_s, t_s), bool)), w_s_a[j][:, :t_s, :t_s], 0.0)
            y = _gate_sample(w_small.reshape(N_GROUPS_A, t_s * t_s), b_s_a[j][:, :t_s], u, vn_s, y, mp, t_s, ns, group)
            v_new.append(vn_s.reshape(t_s, ns, d_a).transpose(1, 0, 2))
            x = _mm_resid(y, w_out_a, j, x, mod, i, 2, mp, 1024, 512, order="nm")
        else:
            gg = _mm_bias_act(h, w_in_b, b_in_b, j, 0, d_rnn, "gelu", BF16)
            xb = _mm_bias_act(h, w_in_b, b_in_b, j, d_rnn, d_rnn, None, F32)
            wra = _pad_block_weights(0.5 * w_ra_b[j], block, blocks_per_group, win)
            wix = _pad_block_weights(0.5 * w_ix_b[j], block, blocks_per_group, win)
            common = (w_conv_b[j], b_conv_b[j], wra, b_ra_b[j], wix, b_ix_b[j], lam_b[j])
            geom = dict(block=block, blocks_per_group=blocks_per_group, win=win)
            y, cp, hp = _rglru(xb, gg, jnp.zeros((halo_p, d_rnn), F32), jnp.zeros((SUBLANES, d_rnn), F32), *common,
                               None, row0=0, nrows=mp, p=nb, bt=512, **geom)
            hist_s = state_conv[j].transpose(1, 0, 2).reshape((CONV_W - 1) * ns, d_rnn)
            y, cs, hs = _rglru(xb, gg, hist_s, state_h[j], *common, y, row0=mp, nrows=ms, p=ns, bt=ns, **geom)
            conv_p.append(cp[halo_p - (CONV_W - 1) * nb:].reshape(CONV_W - 1, nb, d_rnn).transpose(1, 0, 2))
            h_p.append(hp[SUBLANES - nb:])
            conv_s.append(cs.reshape(CONV_W - 1, ns, d_rnn).transpose(1, 0, 2))
            h_s.append(hs)
            x = _mm_resid(y, w_out_b, j, x, mod, i, 2, mp, 1024, 512, order="nm")
        h = _rms_mod(x, g_norm2[i], mod, i, 3, 4, mp)
        f, w2b = _mm_swiglu(h, w_ff1, w_ff3, w_ff2, i)
        x = _mm_resid(f, w2b[None], 0, x, mod, i, 5, mp, 512, 512, order="nm")

    y_prompt = _rms_seq_major(x, g_final, nb, t_p)
    y_sample = _rms(x, g_final, mp, ms).reshape(t_s, ns, d).transpose(1, 0, 2)
    return (y_prompt, y_sample, jnp.stack(v_new), jnp.stack(conv_p), jnp.stack(h_p), jnp.stack(conv_s),
            jnp.stack(h_s))
```

```python
import functools
import math

import jax
import jax.numpy as jnp
from jax import lax
from jax.experimental import pallas as pl
from jax.experimental.pallas import tpu as pltpu

EPS = 1e-6
LRU_C = 8.0
CHUNK = 128
N_GROUPS_A = 16
N_BLOCKS_B = 16
CONV_W = 4

LANES = 128
SUBLANES = 8
PAT = 128
VMEM_LIMIT_BYTES = 56 * 1024 * 1024

F32 = jnp.float32
BF16 = jnp.bfloat16


def _params(*sem):
    return pltpu.CompilerParams(dimension_semantics=sem, vmem_limit_bytes=VMEM_LIMIT_BYTES)


def _sigmoid(x):
    return 0.5 * (jnp.tanh(0.5 * x) + 1.0)


def _ada_kernel(c_ref, w_ref, b_ref, o_ref, act_ref):
    @pl.when((pl.program_id(0) == 0) & (pl.program_id(1) == 0))
    def _():
        c = c_ref[...]
        act_ref[...] = (c * _sigmoid(c)).astype(BF16)

    w = w_ref[...].astype(BF16)
    o_ref[...] = jnp.dot(act_ref[...], w, preferred_element_type=F32) + b_ref[...]


def _ada(c_pat, w_ada, b_ada, bn=1024):
    depth, d, n6 = w_ada.shape
    rows = c_pat.shape[0]
    return pl.pallas_call(
        _ada_kernel,
        grid=(depth, n6 // bn),
        in_specs=[
            pl.BlockSpec((rows, d), lambda i, n: (0, 0)),
            pl.BlockSpec((None, d, bn), lambda i, n: (i, 0, n)),
            pl.BlockSpec((None, 1, bn), lambda i, n: (i, 0, n)),
        ],
        out_specs=pl.BlockSpec((None, rows, bn), lambda i, n: (i, 0, n)),
        out_shape=jax.ShapeDtypeStruct((depth, rows, n6), F32),
        scratch_shapes=[pltpu.VMEM((rows, d), BF16)],
        compiler_params=_params("arbitrary", "arbitrary"),
        name="ada",
    )(c_pat, w_ada, b_ada.reshape(depth, 1, n6))


def _rms_mod_kernel(x_ref, g_ref, sh_ref, sc_ref, o_ref):
    g = g_ref[...]
    scale = 1.0 + sc_ref[...]
    shift = sh_ref[...]
    for k in range(x_ref.shape[0] // PAT):
        rows = slice(k * PAT, (k + 1) * PAT)
        x = x_ref[rows, :]
        y = x * lax.rsqrt(jnp.mean(x * x, axis=-1, keepdims=True) + EPS) * g
        o_ref[rows, :] = (y * scale + shift).astype(o_ref.dtype)


def _rms_mod(x, g, mod, layer, piece_shift, piece_scale, mp, bm=512):
    m, d = x.shape
    n_prompt_tiles = mp // bm

    def pat(piece):
        return lambda i: (layer, jnp.where(i >= n_prompt_tiles, 1, 0), piece)

    return pl.pallas_call(
        _rms_mod_kernel,
        grid=(m // bm,),
        in_specs=[
            pl.BlockSpec((bm, d), lambda i: (i, 0)),
            pl.BlockSpec((1, d), lambda i: (0, 0)),
            pl.BlockSpec((None, PAT, d), pat(piece_shift)),
            pl.BlockSpec((None, PAT, d), pat(piece_scale)),
        ],
        out_specs=pl.BlockSpec((bm, d), lambda i: (i, 0)),
        out_shape=jax.ShapeDtypeStruct((m, d), BF16),
        compiler_params=_params("arbitrary"),
        name="rms_mod",
    )(x, g.reshape(1, d), mod, mod)


def _rms_kernel(x_ref, g_ref, o_ref):
    x = x_ref[...]
    o_ref[...] = x * lax.rsqrt(jnp.mean(x * x, axis=-1, keepdims=True) + EPS) * g_ref[...]


def _rms(x, g, row0, nrows, bm=256):
    d = x.shape[1]
    off = row0 // bm
    return pl.pallas_call(
        _rms_kernel,
        grid=(nrows // bm,),
        in_specs=[pl.BlockSpec((bm, d), lambda i: (i + off, 0)), pl.BlockSpec((1, d), lambda i: (0, 0))],
        out_specs=pl.BlockSpec((bm, d), lambda i: (i, 0)),
        out_shape=jax.ShapeDtypeStruct((nrows, d), F32),
        compiler_params=_params("arbitrary"),
        name="rms_final",
    )(x, g.reshape(1, d))


def _rms_seq_major_kernel(x_ref, g_ref, o_ref, s_ref):
    nb, bt, d = o_ref.shape
    x = x_ref[...]
    y = x * lax.rsqrt(jnp.mean(x * x, axis=-1, keepdims=True) + EPS) * g_ref[...]
    for c in range(d // LANES):
        s_ref[c] = y[:, c * LANES:(c + 1) * LANES]
    for c in range(d // LANES):
        for s in range(nb):
            o_ref[s, :, c * LANES:(c + 1) * LANES] = s_ref[c, pl.ds(s, bt, stride=nb), :]


def _rms_seq_major(x, g, nb, t_len, bt=128):
    d = x.shape[1]
    return pl.pallas_call(
        _rms_seq_major_kernel,
        grid=(t_len // bt,),
        in_specs=[pl.BlockSpec((bt * nb, d), lambda i: (i, 0)), pl.BlockSpec((1, d), lambda i: (0, 0))],
        out_specs=pl.BlockSpec((nb, bt, d), lambda i: (0, i, 0)),
        out_shape=jax.ShapeDtypeStruct((nb, t_len, d), F32),
        scratch_shapes=[pltpu.VMEM((d // LANES, bt * nb, LANES), F32)],
        compiler_params=_params("arbitrary"),
        name="rms_final_prompt",
    )(x, g.reshape(1, d))


def _to_time_major_kernel(xp_ref, xs_ref, o_ref, s_ref, *, n_prompt_tiles):
    nb, bt, d = xp_ref.shape
    i = pl.program_id(0)

    @pl.when(i < n_prompt_tiles)
    def _():
        for c in range(d // LANES):
            for s in range(nb):
                s_ref[c, pl.ds(s, bt, stride=nb), :] = xp_ref[s, :, c * LANES:(c + 1) * LANES]
        for c in range(d // LANES):
            o_ref[:, c * LANES:(c + 1) * LANES] = s_ref[c]

    @pl.when(i >= n_prompt_tiles)
    def _():
        o_ref[...] = xs_ref[...]


def _to_time_major(x_prompt, xs_tm, bt=128):
    nb, t_len, d = x_prompt.shape
    ms = xs_tm.shape[0]
    rows = bt * nb
    n_prompt_tiles = t_len // bt
    return pl.pallas_call(
        functools.partial(_to_time_major_kernel, n_prompt_tiles=n_prompt_tiles),
        grid=(n_prompt_tiles + ms // rows,),
        in_specs=[
            pl.BlockSpec((nb, bt, d), lambda i: (0, jnp.minimum(i, n_prompt_tiles - 1), 0)),
            pl.BlockSpec((rows, d), lambda i: (jnp.maximum(i - n_prompt_tiles, 0), 0)),
        ],
        out_specs=pl.BlockSpec((rows, d), lambda i: (i, 0)),
        out_shape=jax.ShapeDtypeStruct((nb * t_len + ms, d), F32),
        scratch_shapes=[pltpu.VMEM((d // LANES, rows, LANES), F32)],
        compiler_params=_params("arbitrary"),
        name="to_time_major",
    )(x_prompt, xs_tm)


def _layernorm_kernel(x_ref, g_ref, b_ref, o_ref):
    x = x_ref[...]
    mu = jnp.mean(x, axis=-1, keepdims=True)
    xc = x - mu
    var = jnp.mean(xc * xc, axis=-1, keepdims=True)
    o_ref[...] = (xc * lax.rsqrt(var + EPS) * g_ref[...] + b_ref[...]).astype(o_ref.dtype)


def _layernorm(x, g, b, out_dtype, row0, nrows, bm=256):
    d = x.shape[1]
    off = row0 // bm
    return pl.pallas_call(
        _layernorm_kernel,
        grid=(nrows // bm,),
        in_specs=[
            pl.BlockSpec((bm, d), lambda i: (i + off, 0)),
            pl.BlockSpec((1, d), lambda i: (0, 0)),
            pl.BlockSpec((1, d), lambda i: (0, 0)),
        ],
        out_specs=pl.BlockSpec((bm, d), lambda i: (i, 0)),
        out_shape=jax.ShapeDtypeStruct((nrows, d), out_dtype),
        compiler_params=_params("arbitrary"),
        name="layernorm",
    )(x, g.reshape(1, d), b.reshape(1, d))


def _grid_order(order, n_i, n_j):
    if order == "mn":
        return (n_i, n_j), (lambda f: f)
    return (n_j, n_i), (lambda f: (lambda a, b: f(b, a)))


def _mm_bias_act_kernel(x_ref, w_ref, b_ref, *rest, act, side_chunks):
    o_ref = rest[-2] if side_chunks else rest[0]
    acc = jnp.dot(x_ref[...], w_ref[...].astype(BF16), preferred_element_type=F32) + b_ref[...]
    if act == "gelu":
        acc = jax.nn.gelu(acc)
    o_ref[...] = acc.astype(o_ref.dtype)

    if side_chunks:
        side_ref, _, side_out_ref = rest
        step = pl.program_id(0) * pl.num_programs(1) + pl.program_id(1)

        @pl.when(step < side_chunks)
        def _():
            side_out_ref[...] = side_ref[...].astype(BF16)


def _mm_bias_act(x, w, b, layer, col0, ncols, act, out_dtype, side=None, bm=1024, bn=512):
    m, k = x.shape
    noff = col0 // bn
    nj = ncols // bn
    b3 = b.reshape(b.shape[0], 1, b.shape[1])
    in_specs = [
        pl.BlockSpec((bm, k), lambda i, n: (i, 0)),
        pl.BlockSpec((None, k, bn), lambda i, n: (layer, 0, n + noff)),
        pl.BlockSpec((None, 1, bn), lambda i, n: (layer, 0, n + noff)),
    ]
    out_specs = [pl.BlockSpec((bm, bn), lambda i, n: (i, n))]
    out_shape = [jax.ShapeDtypeStruct((m, ncols), out_dtype)]
    args = [x, w, b3]
    side_chunks = 0
    if side is not None:
        _, sk, sn = side.shape
        side_rows = next(r for r in (128, 256, 512, 1024) if sk % r == 0 and sk // r <= (m // bm) * nj)
        side_chunks = sk // side_rows
        chunk = lambda i, n: jnp.minimum(i * nj + n, side_chunks - 1)
        in_specs.append(pl.BlockSpec((None, side_rows, sn), lambda i, n: (layer, chunk(i, n), 0)))
        out_specs.append(pl.BlockSpec((side_rows, sn), lambda i, n: (chunk(i, n), 0)))
        out_shape.append(jax.ShapeDtypeStruct((sk, sn), BF16))
        args.append(side)
    out = pl.pallas_call(
        functools.partial(_mm_bias_act_kernel, act=act, side_chunks=side_chunks),
        grid=(m // bm, nj),
        in_specs=in_specs,
        out_specs=out_specs,
        out_shape=out_shape,
        compiler_params=_params("arbitrary", "arbitrary"),
        name="mm_bias_act",
    )(*args)
    return out if side is not None else out[0]


def _mm_swiglu_kernel(x_ref, w1_ref, w3_ref, w2_ref, o_ref, w2b_ref):
    x = x_ref[...]
    a = jnp.dot(x, w1_ref[...].astype(BF16), preferred_element_type=F32)
    b = jnp.dot(x, w3_ref[...].astype(BF16), preferred_element_type=F32)
    o_ref[...] = (a * _sigmoid(a) * b).astype(o_ref.dtype)

    @pl.when(pl.program_id(0) == 0)
    def _():
        w2b_ref[...] = w2_ref[...].astype(BF16)


def _mm_swiglu(x, w1, w3, w2, layer, bm=1024, bn=256):
    m, k = x.shape
    n = w1.shape[2]
    nj = n // bn
    rk, n2 = w2.shape[1] // nj, w2.shape[2]
    wspec = pl.BlockSpec((None, k, bn), lambda i, j: (layer, 0, j))
    chunk = lambda i, j: jnp.where(i == 0, j, nj - 1)
    return pl.pallas_call(
        _mm_swiglu_kernel,
        grid=(m // bm, nj),
        in_specs=[
            pl.BlockSpec((bm, k), lambda i, j: (i, 0)), wspec, wspec,
            pl.BlockSpec((None, rk, n2), lambda i, j: (layer, chunk(i, j), 0)),
        ],
        out_specs=[
            pl.BlockSpec((bm, bn), lambda i, j: (i, j)),
            pl.BlockSpec((rk, n2), lambda i, j: (chunk(i, j), 0)),
        ],
        out_shape=[jax.ShapeDtypeStruct((m, n), BF16), jax.ShapeDtypeStruct((nj * rk, n2), BF16)],
        compiler_params=_params("arbitrary", "arbitrary"),
        name="mm_swiglu",
    )(x, w1, w3, w2)


def _mm_resid_kernel(x_ref, w_ref, res_ref, gate_ref, o_ref):
    acc = jnp.dot(x_ref[...], w_ref[...].astype(BF16), preferred_element_type=F32)
    gate = gate_ref[...]
    for k in range(acc.shape[0] // PAT):
        rows = slice(k * PAT, (k + 1) * PAT)
        o_ref[rows, :] = res_ref[rows, :] + gate * acc[rows, :]


def _mm_resid(x, w, layer_w, res, mod, layer, piece_gate, mp, bm, bn, order="mn"):
    m, k = x.shape
    n = w.shape[2]
    n_prompt_tiles = mp // bm
    gate_col0 = piece_gate * (n // bn)
    grid, ix = _grid_order(order, m // bm, n // bn)
    return pl.pallas_call(
        _mm_resid_kernel,
        grid=grid,
        in_specs=[
            pl.BlockSpec((bm, k), ix(lambda i, j: (i, 0))),
            pl.BlockSpec((None, k, bn), ix(lambda i, j: (layer_w, 0, j))),
            pl.BlockSpec((bm, bn), ix(lambda i, j: (i, j))),
            pl.BlockSpec((None, PAT, bn),
                         ix(lambda i, j: (layer, jnp.where(i >= n_prompt_tiles, 1, 0), gate_col0 + j))),
        ],
        out_specs=pl.BlockSpec((bm, bn), ix(lambda i, j: (i, j))),
        out_shape=jax.ShapeDtypeStruct((m, n), F32),
        input_output_aliases={2: 0},
        compiler_params=_params("arbitrary", "arbitrary"),
        name="mm_resid",
    )(x, w, res, mod)


def _gate_prompt_kernel(u_ref, v_ref, gv_ref, bv_ref, w_ref, bias_ref, y_ref, mix_ref, vn_ref, *, group, nb, n_chunks):
    c = pl.program_id(0)
    gpt, r, _ = mix_ref.shape
    chunk = w_ref.shape[1]

    @pl.when(c == 0)
    def _():
        shift = nb.bit_length() - 1
        rep = (lax.shift_right_logical(lax.broadcasted_iota(jnp.int32, (r, chunk), 0), shift)
               == lax.broadcasted_iota(jnp.int32, (r, chunk), 1)).astype(BF16)
        causal = lax.broadcasted_iota(jnp.int32, (chunk, chunk), 0) >= lax.broadcasted_iota(jnp.int32, (chunk, chunk), 1)
        same_seq = ((lax.broadcasted_iota(jnp.int32, (r, r), 0) & (nb - 1))
                    == (lax.broadcasted_iota(jnp.int32, (r, r), 1) & (nb - 1)))
        for g in range(gpt):
            wg = jnp.where(causal, w_ref[g], 0.0).astype(BF16)
            left = jnp.dot(rep, wg, preferred_element_type=F32).astype(BF16)
            full = lax.dot_general(left, rep, (((1,), (1,)), ((), ())), preferred_element_type=F32)
            mix_ref[g] = jnp.where(same_seq, full, 0.0).astype(BF16)

    @pl.when(c < n_chunks)
    def _():
        gain = gv_ref[...]
        shift = bv_ref[...]
        for k in range(r // PAT):
            rows = slice(k * PAT, (k + 1) * PAT)
            x = v_ref[rows, :]
            xc = x - jnp.mean(x, axis=-1, keepdims=True)
            var = jnp.mean(xc * xc, axis=-1, keepdims=True)
            vn_ref[rows, :] = (xc * lax.rsqrt(var + EPS) * gain + shift).astype(BF16)
        for g in range(gpt):
            cols = slice(g * group, (g + 1) * group)
            sv = jnp.dot(mix_ref[g], vn_ref[:, cols], preferred_element_type=F32) + bias_ref[:, g:g + 1]
            y_ref[:, cols] = (u_ref[:, cols].astype(F32) * sv).astype(y_ref.dtype)

    @pl.when(c >= n_chunks)
    def _():
        y_ref[...] = jnp.zeros(y_ref.shape, y_ref.dtype)


def _gate_prompt(u, v, g_v, b_v, w_s, bias_t, mp, group, nb):
    m, d = u.shape
    n_groups, chunk, _ = w_s.shape
    r = chunk * nb
    n_chunks = mp // r
    row_blk = lambda c: (jnp.minimum(c, n_chunks - 1), 0)
    whole = lambda shape: pl.BlockSpec(shape, lambda c: (0,) * len(shape))
    return pl.pallas_call(
        functools.partial(_gate_prompt_kernel, group=group, nb=nb, n_chunks=n_chunks),
        grid=(m // r,),
        in_specs=[
            pl.BlockSpec((r, d), row_blk),
            pl.BlockSpec((r, d), row_blk),
            whole((1, d)), whole((1, d)), whole((n_groups, chunk, chunk)), whole((r, n_groups)),
        ],
        out_specs=pl.BlockSpec((r, d), lambda c: (c, 0)),
        out_shape=jax.ShapeDtypeStruct((m, d), BF16),
        scratch_shapes=[pltpu.VMEM((n_groups, r, r), BF16), pltpu.VMEM((r, d), BF16)],
        compiler_params=_params("arbitrary"),
        name="gate_prompt",
    )(u, v, g_v.reshape(1, d), b_v.reshape(1, d), w_s, bias_t)


def _gate_sample_kernel(w_ref, b_ref, u_ref, v_ref, y_any_ref, y_ref, *, steps, p):
    del y_any_ref
    g = pl.program_id(0)
    for t in range(steps):
        acc = w_ref[g, t * steps] * v_ref[0:p, :]
        for s in range(1, t + 1):
            acc = acc + w_ref[g, t * steps + s] * v_ref[s * p:(s + 1) * p, :]
        acc = acc + b_ref[g, t]
        rows = slice(t * p, (t + 1) * p)
        y_ref[rows, :] = (u_ref[rows, :].astype(F32) * acc).astype(y_ref.dtype)


def _gate_sample(w_small, b_small, u, vn_s, y, mp, steps, p, group):
    ms = steps * p
    n_groups = w_small.shape[0]
    row_blk = mp // ms
    return pl.pallas_call(
        functools.partial(_gate_sample_kernel, steps=steps, p=p),
        grid=(n_groups,),
        in_specs=[
            pl.BlockSpec(memory_space=pltpu.SMEM),
            pl.BlockSpec(memory_space=pltpu.SMEM),
            pl.BlockSpec((ms, group), lambda g: (row_blk, g)),
            pl.BlockSpec((ms, group), lambda g: (0, g)),
            pl.BlockSpec(memory_space=pl.ANY),
        ],
        out_specs=pl.BlockSpec((ms, group), lambda g: (row_blk, g)),
        out_shape=jax.ShapeDtypeStruct(y.shape, y.dtype),
        input_output_aliases={4: 0},
        compiler_params=_params("arbitrary"),
        name="gate_sample",
    )(w_small, b_small, u, vn_s, y)


def _window_starts(block, blocks_per_group, group_w, win):
    return [min((bl * block) // LANES * LANES, group_w - win) for bl in range(blocks_per_group)]


def _rglru_kernel(*refs, n_t, **static):
    y_ref = refs[-(_RGLRU_N_OUT + _RGLRU_N_SCRATCH)]
    t = pl.program_id(1)
    pl.when(t < n_t)(functools.partial(_rglru_tile, *refs, n_t=n_t, **static))

    @pl.when(t >= n_t)
    def _():
        y_ref[...] = jnp.zeros(y_ref.shape, y_ref.dtype)


_RGLRU_N_OUT = 3
_RGLRU_N_SCRATCH = 6


def _rglru_tile(*refs, p, halo, starts, win, aliased, n_t):
    if aliased:
        refs = refs[1:]
    (xb_ref, gg_ref, hist_ref, h0_ref, wc_ref, bc_ref, wra_ref, bra_ref, wix_ref, bix_ref, lam_ref,
     y_ref, conv_ref, hlast_ref, s_ref, xc_ref, xcb_ref, a_ref, bx_ref, hc_ref) = refs
    t = pl.program_id(1)
    bt, gw = xb_ref.shape

    @pl.when(t == 0)
    def _():
        s_ref[0:halo, :] = hist_ref[...]
        hc_ref[...] = h0_ref[...]

    s_ref[halo:halo + bt, :] = xb_ref[...]
    taps = [halo - (CONV_W - 1 - k) * p for k in range(CONV_W)]
    rc = min(bt, 64)

    def conv(r0, cols):
        sl = s_ref[r0:r0 + rc + halo, cols]
        rl = sl if p % SUBLANES == 0 else pltpu.roll(sl, p, 0)
        xc = bc_ref[:, cols]
        for k, lo in enumerate(taps):
            src = sl[lo:lo + rc] if lo % SUBLANES == 0 else rl[lo + p:lo + p + rc]
            xc = xc + wc_ref[k:k + 1, cols] * src
        xc_ref[r0:r0 + rc, cols] = xc
        xcb_ref[r0:r0 + rc, cols] = xc.astype(BF16)

    lam = lam_ref[...]
    c8h = (-0.5 * LRU_C) * (jnp.maximum(-lam, 0.0) + jnp.log1p(jnp.exp(-jnp.abs(lam))))

    rg = min(bt, 32)

    def gates(rows, cols):
        c = c8h[:, cols]
        log_a = c * jnp.tanh(a_ref[rows, cols]) + c
        a = jnp.exp(log_a)
        mult = jnp.sqrt(jnp.tanh(log_a) * (-1.0 - a * a))
        a_ref[rows, cols] = a
        bx_ref[rows, cols] = mult * (xc_ref[rows, cols] * (0.5 * jnp.tanh(bx_ref[rows, cols]) + 0.5))

    a_ref[...] = jnp.broadcast_to(0.5 * bra_ref[...], (bt, gw))
    bx_ref[...] = jnp.broadcast_to(0.5 * bix_ref[...], (bt, gw))
    col0 = conv_done = 0
    for bl, lo in enumerate(starts):
        if lo + win > conv_done:
            for r in range(bt // rc):
                conv(r * rc, slice(conv_done, lo + win))
            conv_done = lo + win
        xw = xcb_ref[:, lo:lo + win]
        a_ref[:, lo:lo + win] += jnp.dot(xw, wra_ref[bl], preferred_element_type=F32)
        bx_ref[:, lo:lo + win] += jnp.dot(xw, wix_ref[bl], preferred_element_type=F32)
        col1 = starts[bl + 1] if bl + 1 < len(starts) else gw
        for r in range(bt // rg):
            gates(slice(r * rg, (r + 1) * rg), slice(col0, col1))
        col0 = col1
    s_ref[0:halo, :] = s_ref[bt:bt + halo, :]

    if p % SUBLANES == 0:
        h = hc_ref[...]
        for step in range(bt // p):
            rows = slice(step * p, (step + 1) * p)
            h = a_ref[rows, :] * h + bx_ref[rows, :]
            y_ref[rows, :] = (gg_ref[rows, :].astype(F32) * h).astype(y_ref.dtype)
        hc_ref[...] = h
    else:
        upper = lax.broadcasted_iota(jnp.int32, (SUBLANES, gw), 0) >= p

        def scan(i, hc):
            rows = pl.ds(pl.multiple_of(i * SUBLANES, SUBLANES), SUBLANES)
            a8 = a_ref[rows, :]
            b8 = bx_ref[rows, :]
            h_lo = a8 * hc + b8
            h_hi = a8 * pltpu.roll(h_lo, p, 0) + b8
            a_ref[rows, :] = jnp.where(upper, h_hi, h_lo)
            return jnp.where(upper, h_hi, pltpu.roll(h_hi, p, 0))

        hc_ref[...] = lax.fori_loop(0, bt // SUBLANES, scan, hc_ref[...], unroll=2)

        def gate_out(i, carry):
            rows = pl.ds(pl.multiple_of(i * rg, rg), rg)
            y_ref[rows, :] = (gg_ref[rows, :].astype(F32) * a_ref[rows, :]).astype(y_ref.dtype)
            return carry

        lax.fori_loop(0, bt // rg, gate_out, 0)

    @pl.when(t == n_t - 1)
    def _():
        conv_ref[...] = s_ref[0:halo, :]
        hlast_ref[...] = hc_ref[...]


def _rglru(xb, gg, hist, h0, wconv, bconv, wra, bra, wix, bix, lam, y_prev, *, row0, nrows, p, bt,
           block, blocks_per_group, win):
    m, c = xb.shape
    gw = block * blocks_per_group
    n_groups = c // gw
    halo = hist.shape[0]
    hrows = h0.shape[0]
    starts = _window_starts(block, blocks_per_group, gw, win)
    rb = row0 // bt
    aliased = y_prev is not None
    n_t = nrows // bt
    n_zero = 0 if aliased else (m - row0 - nrows) // bt

    row_blk = lambda g, t: (rb + t, g)
    in_blk = lambda g, t: (rb + jnp.minimum(t, n_t - 1), g)
    vec = pl.BlockSpec((1, gw), lambda g, t: (0, g))
    wspec = pl.BlockSpec((blocks_per_group, win, win), lambda g, t: (g, 0, 0))
    in_specs = [
        pl.BlockSpec((bt, gw), in_blk),
        pl.BlockSpec((bt, gw), in_blk),
        pl.BlockSpec((halo, gw), lambda g, t: (0, g)),
        pl.BlockSpec((hrows, gw), lambda g, t: (0, g)),
        pl.BlockSpec((CONV_W, gw), lambda g, t: (0, g)),
        vec, wspec, vec, wspec, vec, vec,
    ]
    args = [xb, gg, hist, h0, wconv, bconv.reshape(1, c), wra, bra.reshape(1, c), wix, bix.reshape(1, c),
            lam.reshape(1, c)]
    aliases = {}
    if aliased:
        in_specs = [pl.BlockSpec(memory_space=pl.ANY)] + in_specs
        args = [y_prev] + args
        aliases = {0: 0}
    return pl.pallas_call(
        functools.partial(_rglru_kernel, p=p, halo=halo, starts=starts, win=win, aliased=aliased, n_t=n_t),
        grid=(n_groups, n_t + n_zero),
        in_specs=in_specs,
        out_specs=[
            pl.BlockSpec((bt, gw), row_blk),
            pl.BlockSpec((halo, gw), lambda g, t: (0, g)),
            pl.BlockSpec((hrows, gw), lambda g, t: (0, g)),
        ],
        out_shape=[
            jax.ShapeDtypeStruct((m, c), BF16),
            jax.ShapeDtypeStruct((halo, c), F32),
            jax.ShapeDtypeStruct((hrows, c), F32),
        ],
        scratch_shapes=[
            pltpu.VMEM((halo + bt, gw), F32),
            pltpu.VMEM((bt, gw), F32),
            pltpu.VMEM((bt, gw), BF16),
            pltpu.VMEM((bt, gw), F32),
            pltpu.VMEM((bt, gw), F32),
            pltpu.VMEM((hrows, gw), F32),
        ],
        input_output_aliases=aliases,
        compiler_params=_params("arbitrary", "arbitrary"),
        name="rglru",
    )(*args)


def _pad_block_weights(w, block, blocks_per_group, win):
    gw = block * blocks_per_group
    starts = _window_starts(block, blocks_per_group, gw, win)
    n_groups = w.shape[0] // blocks_per_group
    wg = w.astype(BF16).reshape(n_groups, blocks_per_group, block, block)
    padded = []
    for bl, lo in enumerate(starts):
        off = bl * block - lo
        rest = win - off - block
        padded.append(jnp.pad(wg[:, bl], ((0, 0), (off, rest), (off, rest))))
    return jnp.stack(padded, axis=1).reshape(w.shape[0], win, win)


def kernel(x_prompt, x_sample, state_conv, state_h, c_prompt, c_sample, w_ada, b_ada, g_norm1, g_norm2, g_final,
           w_in_a, b_in_a, g_v_a, b_v_a, w_s_a, b_s_a, w_out_a, w_in_b, b_in_b, w_conv_b, b_conv_b, w_ra_b, b_ra_b,
           w_ix_b, b_ix_b, lam_b, w_out_b, w_ff1, w_ff3, w_ff2):
    nb, t_p, d = x_prompt.shape
    ns, t_s, _ = x_sample.shape
    depth = w_ada.shape[0]
    d_a = w_out_a.shape[1]
    d_rnn = w_out_b.shape[1]
    group = d_a // N_GROUPS_A
    block = d_rnn // N_BLOCKS_B
    mp, ms = nb * t_p, ns * t_s
    assert ns == PAT and PAT % nb == 0 and t_p % CHUNK == 0 and t_s <= CHUNK and nb == 4

    x = _to_time_major(x_prompt, x_sample.transpose(1, 0, 2).reshape(ms, d))
    c_pat = jnp.concatenate([jnp.tile(c_prompt, (PAT // nb, 1)), c_sample])
    mod = _ada(c_pat, w_ada, b_ada)


    blocks_per_group = LANES // math.gcd(block, LANES)
    win = (-(-block // LANES) + 1) * LANES
    halo_p = -(-(CONV_W - 1) * nb // SUBLANES) * SUBLANES

    v_new, conv_p, h_p, conv_s, h_s = [], [], [], [], []
    for i in range(depth):
        j = i // 2
        h = _rms_mod(x, g_norm1[i], mod, i, 0, 1, mp)
        if i % 2 == 0:
            u = _mm_bias_act(h, w_in_a, b_in_a, j, 0, d_a, "gelu", BF16)
            v = _mm_bias_act(h, w_in_a, b_in_a, j, d_a, d_a, "gelu", F32)
            vn_s = _layernorm(v, g_v_a[j], b_v_a[j], F32, mp, ms)
            bias_t = jnp.repeat(b_s_a[j], nb, axis=1).T
            y = _gate_prompt(u, v, g_v_a[j], b_v_a[j], w_s_a[j], bias_t, mp, group, nb)
            w_small = jnp.where(jnp.tril(jnp.ones((t_s, t_s), bool)), w_s_a[j][:, :t_s, :t_s], 0.0)
            y = _gate_sample(w_small.reshape(N_GROUPS_A, t_s * t_s), b_s_a[j][:, :t_s], u, vn_s, y, mp, t_s, ns, group)
            v_new.append(vn_s.reshape(t_s, ns, d_a).transpose(1, 0, 2))
            x = _mm_resid(y, w_out_a, j, x, mod, i, 2, mp, 1024, 512, order="nm")
        else:
            gg, w_out_bf = _mm_bias_act(h, w_in_b, b_in_b, j, 0, d_rnn, "gelu", BF16, side=w_out_b)
            xb = _mm_bias_act(h, w_in_b, b_in_b, j, d_rnn, d_rnn, None, F32)
            wra = _pad_block_weights(0.5 * w_ra_b[j], block, blocks_per_group, win)
            wix = _pad_block_weights(0.5 * w_ix_b[j], block, blocks_per_group, win)
            common = (w_conv_b[j], b_conv_b[j], wra, b_ra_b[j], wix, b_ix_b[j], lam_b[j])
            geom = dict(block=block, blocks_per_group=blocks_per_group, win=win)
            y, cp, hp = _rglru(xb, gg, jnp.zeros((halo_p, d_rnn), F32), jnp.zeros((SUBLANES, d_rnn), F32), *common,
                               None, row0=0, nrows=mp, p=nb, bt=512, **geom)
            hist_s = state_conv[j].transpose(1, 0, 2).reshape((CONV_W - 1) * ns, d_rnn)
            y, cs, hs = _rglru(xb, gg, hist_s, state_h[j], *common, y, row0=mp, nrows=ms, p=ns, bt=2 * ns, **geom)
            conv_p.append(cp[halo_p - (CONV_W - 1) * nb:].reshape(CONV_W - 1, nb, d_rnn).transpose(1, 0, 2))
            h_p.append(hp[SUBLANES - nb:])
            conv_s.append(cs.reshape(CONV_W - 1, ns, d_rnn).transpose(1, 0, 2))
            h_s.append(hs)
            x = _mm_resid(y, w_out_bf[None], 0, x, mod, i, 2, mp, 1024, 512, order="nm")
        h = _rms_mod(x, g_norm2[i], mod, i, 3, 4, mp, bm=256)
        f, w2b = _mm_swiglu(h, w_ff1, w_ff3, w_ff2, i)
        x = _mm_resid(f, w2b[None], 0, x, mod, i, 5, mp, 512, 512, order="nm")

    y_prompt = _rms_seq_major(x, g_final, nb, t_p)
    y_sample = _rms(x, g_final, mp, ms).reshape(t_s, ns, d).transpose(1, 0, 2)
    return (y_prompt, y_sample, jnp.stack(v_new), jnp.stack(conv_p), jnp.stack(h_p), jnp.stack(conv_s),
            jnp.stack(h_s))
```

```python
import functools
import math

import jax
import jax.numpy as jnp
from jax import lax
from jax.experimental import pallas as pl
from jax.experimental.pallas import tpu as pltpu

EPS = 1e-6
LRU_C = 8.0
CHUNK = 128
N_GROUPS_A = 16
N_BLOCKS_B = 16
CONV_W = 4

LANES = 128
SUBLANES = 8
PAT = 128
VMEM_LIMIT_BYTES = 56 * 1024 * 1024

F32 = jnp.float32
BF16 = jnp.bfloat16


def _params(*sem):
    return pltpu.CompilerParams(dimension_semantics=sem, vmem_limit_bytes=VMEM_LIMIT_BYTES)


def _sigmoid(x):
    return 0.5 * (jnp.tanh(0.5 * x) + 1.0)


def _ada_kernel(c_ref, w_ref, b_ref, o_ref, act_ref):
    @pl.when((pl.program_id(0) == 0) & (pl.program_id(1) == 0))
    def _():
        c = c_ref[...]
        act_ref[...] = (c * _sigmoid(c)).astype(BF16)

    w = w_ref[...].astype(BF16)
    o_ref[...] = jnp.dot(act_ref[...], w, preferred_element_type=F32) + b_ref[...]


def _ada(c_pat, w_ada, b_ada, bn=1024):
    depth, d, n6 = w_ada.shape
    rows = c_pat.shape[0]
    return pl.pallas_call(
        _ada_kernel,
        grid=(depth, n6 // bn),
        in_specs=[
            pl.BlockSpec((rows, d), lambda i, n: (0, 0)),
            pl.BlockSpec((None, d, bn), lambda i, n: (i, 0, n)),
            pl.BlockSpec((None, 1, bn), lambda i, n: (i, 0, n)),
        ],
        out_specs=pl.BlockSpec((None, rows, bn), lambda i, n: (i, 0, n)),
        out_shape=jax.ShapeDtypeStruct((depth, rows, n6), F32),
        scratch_shapes=[pltpu.VMEM((rows, d), BF16)],
        compiler_params=_params("arbitrary", "arbitrary"),
        name="ada",
    )(c_pat, w_ada, b_ada.reshape(depth, 1, n6))


COL_CHUNK = 4 * LANES


def _row_sum(ref, rows, fn):
    acc = None
    for c in range(ref.shape[1] // LANES):
        t = fn(ref[rows, c * LANES:(c + 1) * LANES])
        acc = t if acc is None else acc + t
    return jnp.sum(acc, axis=-1, keepdims=True)


def _col_chunks(d):
    return [slice(c, min(c + COL_CHUNK, d)) for c in range(0, d, COL_CHUNK)]


def _rms_mod_kernel(x_ref, g_ref, sh_ref, sc_ref, o_ref, gain_ref):
    d = x_ref.shape[1]
    gain_ref[...] = g_ref[...] * (1.0 + sc_ref[...])
    for k in range(x_ref.shape[0] // PAT):
        rows = slice(k * PAT, (k + 1) * PAT)
        rstd = lax.rsqrt(_row_sum(x_ref, rows, lambda t: t * t) * (1.0 / d) + EPS)
        for cols in _col_chunks(d):
            o_ref[rows, cols] = ((x_ref[rows, cols] * rstd) * gain_ref[:, cols] + sh_ref[:, cols]).astype(o_ref.dtype)


def _rms_mod(x, g, mod, layer, piece_shift, piece_scale, mp, bm=512):
    m, d = x.shape
    n_prompt_tiles = mp // bm

    def pat(piece):
        return lambda i: (layer, jnp.where(i >= n_prompt_tiles, 1, 0), piece)

    return pl.pallas_call(
        _rms_mod_kernel,
        grid=(m // bm,),
        in_specs=[
            pl.BlockSpec((bm, d), lambda i: (i, 0)),
            pl.BlockSpec((1, d), lambda i: (0, 0)),
            pl.BlockSpec((None, PAT, d), pat(piece_shift)),
            pl.BlockSpec((None, PAT, d), pat(piece_scale)),
        ],
        out_specs=pl.BlockSpec((bm, d), lambda i: (i, 0)),
        out_shape=jax.ShapeDtypeStruct((m, d), BF16),
        scratch_shapes=[pltpu.VMEM((PAT, d), F32)],
        compiler_params=_params("arbitrary"),
        name="rms_mod",
    )(x, g.reshape(1, d), mod, mod)


def _rms_kernel(x_ref, g_ref, o_ref):
    x = x_ref[...]
    o_ref[...] = x * lax.rsqrt(jnp.mean(x * x, axis=-1, keepdims=True) + EPS) * g_ref[...]


def _rms(x, g, row0, nrows, bm=256):
    d = x.shape[1]
    off = row0 // bm
    return pl.pallas_call(
        _rms_kernel,
        grid=(nrows // bm,),
        in_specs=[pl.BlockSpec((bm, d), lambda i: (i + off, 0)), pl.BlockSpec((1, d), lambda i: (0, 0))],
        out_specs=pl.BlockSpec((bm, d), lambda i: (i, 0)),
        out_shape=jax.ShapeDtypeStruct((nrows, d), F32),
        compiler_params=_params("arbitrary"),
        name="rms_final",
    )(x, g.reshape(1, d))


def _rms_seq_major_kernel(x_ref, g_ref, o_ref, s_ref):
    nb, bt, d = o_ref.shape

    for k in range(x_ref.shape[0] // PAT):
        rows = slice(k * PAT, (k + 1) * PAT)
        rstd = lax.rsqrt(_row_sum(x_ref, rows, lambda t: t * t) * (1.0 / d) + EPS)
        for c in range(d // LANES):
            cols = slice(c * LANES, (c + 1) * LANES)
            s_ref[c, rows, :] = x_ref[rows, cols] * rstd * g_ref[:, cols]
    for c in range(d // LANES):
        for s in range(nb):
            o_ref[s, :, c * LANES:(c + 1) * LANES] = s_ref[c, pl.ds(s, bt, stride=nb), :]


def _rms_seq_major(x, g, nb, t_len, bt=128):
    d = x.shape[1]
    return pl.pallas_call(
        _rms_seq_major_kernel,
        grid=(t_len // bt,),
        in_specs=[pl.BlockSpec((bt * nb, d), lambda i: (i, 0)), pl.BlockSpec((1, d), lambda i: (0, 0))],
        out_specs=pl.BlockSpec((nb, bt, d), lambda i: (0, i, 0)),
        out_shape=jax.ShapeDtypeStruct((nb, t_len, d), F32),
        scratch_shapes=[pltpu.VMEM((d // LANES, bt * nb, LANES), F32)],
        compiler_params=_params("arbitrary"),
        name="rms_final_prompt",
    )(x, g.reshape(1, d))


def _to_time_major_kernel(xp_ref, xs_ref, o_ref, s_ref, *, n_prompt_tiles):
    nb, bt, d = xp_ref.shape
    i = pl.program_id(0)

    @pl.when(i < n_prompt_tiles)
    def _():
        for c in range(d // LANES):
            for s in range(nb):
                s_ref[c, pl.ds(s, bt, stride=nb), :] = xp_ref[s, :, c * LANES:(c + 1) * LANES]
        for c in range(d // LANES):
            o_ref[:, c * LANES:(c + 1) * LANES] = s_ref[c]

    @pl.when(i >= n_prompt_tiles)
    def _():
        o_ref[...] = xs_ref[...]


def _to_time_major(x_prompt, xs_tm, bt=128):
    nb, t_len, d = x_prompt.shape
    ms = xs_tm.shape[0]
    rows = bt * nb
    n_prompt_tiles = t_len // bt
    return pl.pallas_call(
        functools.partial(_to_time_major_kernel, n_prompt_tiles=n_prompt_tiles),
        grid=(n_prompt_tiles + ms // rows,),
        in_specs=[
            pl.BlockSpec((nb, bt, d), lambda i: (0, jnp.minimum(i, n_prompt_tiles - 1), 0)),
            pl.BlockSpec((rows, d), lambda i: (jnp.maximum(i - n_prompt_tiles, 0), 0)),
        ],
        out_specs=pl.BlockSpec((rows, d), lambda i: (i, 0)),
        out_shape=jax.ShapeDtypeStruct((nb * t_len + ms, d), F32),
        scratch_shapes=[pltpu.VMEM((d // LANES, rows, LANES), F32)],
        compiler_params=_params("arbitrary"),
        name="to_time_major",
    )(x_prompt, xs_tm)


def _layernorm_kernel(x_ref, g_ref, b_ref, o_ref):
    x = x_ref[...]
    mu = jnp.mean(x, axis=-1, keepdims=True)
    xc = x - mu
    var = jnp.mean(xc * xc, axis=-1, keepdims=True)
    o_ref[...] = (xc * lax.rsqrt(var + EPS) * g_ref[...] + b_ref[...]).astype(o_ref.dtype)


def _layernorm(x, g, b, out_dtype, row0, nrows, bm=256):
    d = x.shape[1]
    off = row0 // bm
    return pl.pallas_call(
        _layernorm_kernel,
        grid=(nrows // bm,),
        in_specs=[
            pl.BlockSpec((bm, d), lambda i: (i + off, 0)),
            pl.BlockSpec((1, d), lambda i: (0, 0)),
            pl.BlockSpec((1, d), lambda i: (0, 0)),
        ],
        out_specs=pl.BlockSpec((bm, d), lambda i: (i, 0)),
        out_shape=jax.ShapeDtypeStruct((nrows, d), out_dtype),
        compiler_params=_params("arbitrary"),
        name="layernorm",
    )(x, g.reshape(1, d), b.reshape(1, d))


def _grid_order(order, n_i, n_j):
    if order == "mn":
        return (n_i, n_j), (lambda f: f)
    return (n_j, n_i), (lambda f: (lambda a, b: f(b, a)))


def _mm_bias_act_kernel(x_ref, w_ref, b_ref, *rest, act, side_chunks):
    o_ref = rest[-2] if side_chunks else rest[0]
    acc = jnp.dot(x_ref[...], w_ref[...].astype(BF16), preferred_element_type=F32) + b_ref[...]
    if act == "gelu":
        acc = jax.nn.gelu(acc)
    o_ref[...] = acc.astype(o_ref.dtype)

    if side_chunks:
        side_ref, _, side_out_ref = rest
        step = pl.program_id(0) * pl.num_programs(1) + pl.program_id(1)

        @pl.when(step < side_chunks)
        def _():
            side_out_ref[...] = side_ref[...].astype(BF16)


def _mm_bias_act(x, w, b, layer, col0, ncols, act, out_dtype, side=None, bm=1024, bn=512):
    m, k = x.shape
    noff = col0 // bn
    nj = ncols // bn
    b3 = b.reshape(b.shape[0], 1, b.shape[1])
    in_specs = [
        pl.BlockSpec((bm, k), lambda i, n: (i, 0)),
        pl.BlockSpec((None, k, bn), lambda i, n: (layer, 0, n + noff)),
        pl.BlockSpec((None, 1, bn), lambda i, n: (layer, 0, n + noff)),
    ]
    out_specs = [pl.BlockSpec((bm, bn), lambda i, n: (i, n))]
    out_shape = [jax.ShapeDtypeStruct((m, ncols), out_dtype)]
    args = [x, w, b3]
    side_chunks = 0
    if side is not None:
        _, sk, sn = side.shape
        side_rows = next(r for r in (128, 256, 512, 1024) if sk % r == 0 and sk // r <= (m // bm) * nj)
        side_chunks = sk // side_rows
        chunk = lambda i, n: jnp.minimum(i * nj + n, side_chunks - 1)
        in_specs.append(pl.BlockSpec((None, side_rows, sn), lambda i, n: (layer, chunk(i, n), 0)))
        out_specs.append(pl.BlockSpec((side_rows, sn), lambda i, n: (chunk(i, n), 0)))
        out_shape.append(jax.ShapeDtypeStruct((sk, sn), BF16))
        args.append(side)
    out = pl.pallas_call(
        functools.partial(_mm_bias_act_kernel, act=act, side_chunks=side_chunks),
        grid=(m // bm, nj),
        in_specs=in_specs,
        out_specs=out_specs,
        out_shape=out_shape,
        compiler_params=_params("arbitrary", "arbitrary"),
        name="mm_bias_act",
    )(*args)
    return out if side is not None else out[0]


def _mm_swiglu_kernel(x_ref, w1_ref, w3_ref, w2_ref, o_ref, w2b_ref):
    x = x_ref[...]
    a = jnp.dot(x, w1_ref[...].astype(BF16), preferred_element_type=F32)
    b = jnp.dot(x, w3_ref[...].astype(BF16), preferred_element_type=F32)
    o_ref[...] = (a * _sigmoid(a) * b).astype(o_ref.dtype)

    @pl.when(pl.program_id(0) == 0)
    def _():
        w2b_ref[...] = w2_ref[...].astype(BF16)


def _mm_swiglu(x, w1, w3, w2, layer, bm=1024, bn=256):
    m, k = x.shape
    n = w1.shape[2]
    nj = n // bn
    rk, n2 = w2.shape[1] // nj, w2.shape[2]
    wspec = pl.BlockSpec((None, k, bn), lambda i, j: (layer, 0, j))
    chunk = lambda i, j: jnp.where(i == 0, j, nj - 1)
    return pl.pallas_call(
        _mm_swiglu_kernel,
        grid=(m // bm, nj),
        in_specs=[
            pl.BlockSpec((bm, k), lambda i, j: (i, 0)), wspec, wspec,
            pl.BlockSpec((None, rk, n2), lambda i, j: (layer, chunk(i, j), 0)),
        ],
        out_specs=[
            pl.BlockSpec((bm, bn), lambda i, j: (i, j)),
            pl.BlockSpec((rk, n2), lambda i, j: (chunk(i, j), 0)),
        ],
        out_shape=[jax.ShapeDtypeStruct((m, n), BF16), jax.ShapeDtypeStruct((nj * rk, n2), BF16)],
        compiler_params=_params("arbitrary", "arbitrary"),
        name="mm_swiglu",
    )(x, w1, w3, w2)


def _mm_resid_kernel(x_ref, w_ref, res_ref, gate_ref, o_ref):
    acc = jnp.dot(x_ref[...], w_ref[...].astype(BF16), preferred_element_type=F32)
    gate = gate_ref[...]
    for k in range(acc.shape[0] // PAT):
        rows = slice(k * PAT, (k + 1) * PAT)
        o_ref[rows, :] = res_ref[rows, :] + gate * acc[rows, :]


def _mm_resid(x, w, layer_w, res, mod, layer, piece_gate, mp, bm, bn, order="mn"):
    m, k = x.shape
    n = w.shape[2]
    n_prompt_tiles = mp // bm
    gate_col0 = piece_gate * (n // bn)
    grid, ix = _grid_order(order, m // bm, n // bn)
    return pl.pallas_call(
        _mm_resid_kernel,
        grid=grid,
        in_specs=[
            pl.BlockSpec((bm, k), ix(lambda i, j: (i, 0))),
            pl.BlockSpec((None, k, bn), ix(lambda i, j: (layer_w, 0, j))),
            pl.BlockSpec((bm, bn), ix(lambda i, j: (i, j))),
            pl.BlockSpec((None, PAT, bn),
                         ix(lambda i, j: (layer, jnp.where(i >= n_prompt_tiles, 1, 0), gate_col0 + j))),
        ],
        out_specs=pl.BlockSpec((bm, bn), ix(lambda i, j: (i, j))),
        out_shape=jax.ShapeDtypeStruct((m, n), F32),
        input_output_aliases={2: 0},
        compiler_params=_params("arbitrary", "arbitrary"),
        name="mm_resid",
    )(x, w, res, mod)


def _gate_prompt_kernel(u_ref, v_ref, gv_ref, bv_ref, w_ref, bias_ref, y_ref, mix_ref, vn_ref, *, group, nb, n_chunks):
    c = pl.program_id(0)
    gpt, r, _ = mix_ref.shape
    chunk = w_ref.shape[1]

    @pl.when(c == 0)
    def _():
        shift = nb.bit_length() - 1
        rep = (lax.shift_right_logical(lax.broadcasted_iota(jnp.int32, (r, chunk), 0), shift)
               == lax.broadcasted_iota(jnp.int32, (r, chunk), 1)).astype(BF16)
        causal = lax.broadcasted_iota(jnp.int32, (chunk, chunk), 0) >= lax.broadcasted_iota(jnp.int32, (chunk, chunk), 1)
        same_seq = ((lax.broadcasted_iota(jnp.int32, (r, r), 0) & (nb - 1))
                    == (lax.broadcasted_iota(jnp.int32, (r, r), 1) & (nb - 1)))
        for g in range(gpt):
            wg = jnp.where(causal, w_ref[g], 0.0).astype(BF16)
            left = jnp.dot(rep, wg, preferred_element_type=F32).astype(BF16)
            full = lax.dot_general(left, rep, (((1,), (1,)), ((), ())), preferred_element_type=F32)
            mix_ref[g] = jnp.where(same_seq, full, 0.0).astype(BF16)

    @pl.when(c < n_chunks)
    def _():
        d = v_ref.shape[1]
        for k in range(r // PAT):
            rows = slice(k * PAT, (k + 1) * PAT)
            mu = _row_sum(v_ref, rows, lambda t: t) * (1.0 / d)
            var = _row_sum(v_ref, rows, lambda t: (t - mu) * (t - mu)) * (1.0 / d)
            rstd = lax.rsqrt(var + EPS)
            for cols in _col_chunks(d):
                vn_ref[rows, cols] = ((v_ref[rows, cols] - mu) * rstd * gv_ref[:, cols] + bv_ref[:, cols]).astype(BF16)
        for g in range(gpt):
            cols = slice(g * group, (g + 1) * group)
            sv = jnp.dot(mix_ref[g], vn_ref[:, cols], preferred_element_type=F32) + bias_ref[:, g:g + 1]
            y_ref[:, cols] = (u_ref[:, cols].astype(F32) * sv).astype(y_ref.dtype)

    @pl.when(c >= n_chunks)
    def _():
        y_ref[...] = jnp.zeros(y_ref.shape, y_ref.dtype)


def _gate_prompt(u, v, g_v, b_v, w_s, bias_t, mp, group, nb):
    m, d = u.shape
    n_groups, chunk, _ = w_s.shape
    r = chunk * nb
    n_chunks = mp // r
    row_blk = lambda c: (jnp.minimum(c, n_chunks - 1), 0)
    whole = lambda shape: pl.BlockSpec(shape, lambda c: (0,) * len(shape))
    return pl.pallas_call(
        functools.partial(_gate_prompt_kernel, group=group, nb=nb, n_chunks=n_chunks),
        grid=(m // r,),
        in_specs=[
            pl.BlockSpec((r, d), row_blk),
            pl.BlockSpec((r, d), row_blk),
            whole((1, d)), whole((1, d)), whole((n_groups, chunk, chunk)), whole((r, n_groups)),
        ],
        out_specs=pl.BlockSpec((r, d), lambda c: (c, 0)),
        out_shape=jax.ShapeDtypeStruct((m, d), BF16),
        scratch_shapes=[pltpu.VMEM((n_groups, r, r), BF16), pltpu.VMEM((r, d), BF16)],
        compiler_params=_params("arbitrary"),
        name="gate_prompt",
    )(u, v, g_v.reshape(1, d), b_v.reshape(1, d), w_s, bias_t)


def _gate_sample_kernel(w_ref, b_ref, u_ref, v_ref, y_any_ref, y_ref, *, steps, p):
    del y_any_ref
    g = pl.program_id(0)
    for t in range(steps):
        acc = w_ref[g, t * steps] * v_ref[0:p, :]
        for s in range(1, t + 1):
            acc = acc + w_ref[g, t * steps + s] * v_ref[s * p:(s + 1) * p, :]
        acc = acc + b_ref[g, t]
        rows = slice(t * p, (t + 1) * p)
        y_ref[rows, :] = (u_ref[rows, :].astype(F32) * acc).astype(y_ref.dtype)


def _gate_sample(w_small, b_small, u, vn_s, y, mp, steps, p, group):
    ms = steps * p
    n_groups = w_small.shape[0]
    row_blk = mp // ms
    return pl.pallas_call(
        functools.partial(_gate_sample_kernel, steps=steps, p=p),
        grid=(n_groups,),
        in_specs=[
            pl.BlockSpec(memory_space=pltpu.SMEM),
            pl.BlockSpec(memory_space=pltpu.SMEM),
            pl.BlockSpec((ms, group), lambda g: (row_blk, g)),
            pl.BlockSpec((ms, group), lambda g: (0, g)),
            pl.BlockSpec(memory_space=pl.ANY),
        ],
        out_specs=pl.BlockSpec((ms, group), lambda g: (row_blk, g)),
        out_shape=jax.ShapeDtypeStruct(y.shape, y.dtype),
        input_output_aliases={4: 0},
        compiler_params=_params("arbitrary"),
        name="gate_sample",
    )(w_small, b_small, u, vn_s, y)


def _window_starts(block, blocks_per_group, group_w, win):
    return [min((bl * block) // LANES * LANES, group_w - win) for bl in range(blocks_per_group)]


def _rglru_kernel(*refs, n_t, **static):
    y_ref = refs[-(_RGLRU_N_OUT + _RGLRU_N_SCRATCH)]
    t = pl.program_id(1)
    pl.when(t < n_t)(functools.partial(_rglru_tile, *refs, n_t=n_t, **static))

    @pl.when(t >= n_t)
    def _():
        y_ref[...] = jnp.zeros(y_ref.shape, y_ref.dtype)


_RGLRU_N_OUT = 3
_RGLRU_N_SCRATCH = 6


def _rglru_tile(*refs, p, halo, starts, win, aliased, n_t):
    if aliased:
        refs = refs[1:]
    (xb_ref, gg_ref, hist_ref, h0_ref, wc_ref, bc_ref, wra_ref, bra_ref, wix_ref, bix_ref, lam_ref,
     y_ref, conv_ref, hlast_ref, s_ref, xc_ref, xcb_ref, a_ref, bx_ref, hc_ref) = refs
    t = pl.program_id(1)
    bt, gw = xb_ref.shape

    @pl.when(t == 0)
    def _():
        s_ref[0:halo, :] = hist_ref[...]
        hc_ref[...] = h0_ref[...]

    s_ref[halo:halo + bt, :] = xb_ref[...]
    taps = [halo - (CONV_W - 1 - k) * p for k in range(CONV_W)]
    rc = min(bt, 64)

    def conv(r0, cols):
        sl = s_ref[r0:r0 + rc + halo, cols]
        rl = sl if p % SUBLANES == 0 else pltpu.roll(sl, p, 0)
        xc = bc_ref[:, cols]
        for k, lo in enumerate(taps):
            src = sl[lo:lo + rc] if lo % SUBLANES == 0 else rl[lo + p:lo + p + rc]
            xc = xc + wc_ref[k:k + 1, cols] * src
        xc_ref[r0:r0 + rc, cols] = xc
        xcb_ref[r0:r0 + rc, cols] = xc.astype(BF16)

    lam = lam_ref[...]
    c8h = (-0.5 * LRU_C) * (jnp.maximum(-lam, 0.0) + jnp.log1p(jnp.exp(-jnp.abs(lam))))

    rg = min(bt, 32)

    def gates(rows, cols):
        c = c8h[:, cols]
        log_a = c * jnp.tanh(a_ref[rows, cols]) + c
        a = jnp.exp(log_a)
        mult = jnp.sqrt(jnp.tanh(log_a) * (-1.0 - a * a))
        a_ref[rows, cols] = a
        bx_ref[rows, cols] = mult * (xc_ref[rows, cols] * (0.5 * jnp.tanh(bx_ref[rows, cols]) + 0.5))

    a_ref[...] = jnp.broadcast_to(0.5 * bra_ref[...], (bt, gw))
    bx_ref[...] = jnp.broadcast_to(0.5 * bix_ref[...], (bt, gw))
    col0 = conv_done = 0
    for bl, lo in enumerate(starts):
        if lo + win > conv_done:
            for r in range(bt // rc):
                conv(r * rc, slice(conv_done, lo + win))
            conv_done = lo + win
        xw = xcb_ref[:, lo:lo + win]
        a_ref[:, lo:lo + win] += jnp.dot(xw, wra_ref[bl], preferred_element_type=F32)
        bx_ref[:, lo:lo + win] += jnp.dot(xw, wix_ref[bl], preferred_element_type=F32)
        col1 = starts[bl + 1] if bl + 1 < len(starts) else gw
        for r in range(bt // rg):
            gates(slice(r * rg, (r + 1) * rg), slice(col0, col1))
        col0 = col1
    s_ref[0:halo, :] = s_ref[bt:bt + halo, :]

    if p % SUBLANES == 0:
        h = hc_ref[...]
        for step in range(bt // p):
            rows = slice(step * p, (step + 1) * p)
            h = a_ref[rows, :] * h + bx_ref[rows, :]
            y_ref[rows, :] = (gg_ref[rows, :].astype(F32) * h).astype(y_ref.dtype)
        hc_ref[...] = h
    else:
        upper = lax.broadcasted_iota(jnp.int32, (SUBLANES, gw), 0) >= p

        def scan(i, hc):
            rows = pl.ds(pl.multiple_of(i * SUBLANES, SUBLANES), SUBLANES)
            a8 = a_ref[rows, :]
            b8 = bx_ref[rows, :]
            h_lo = a8 * hc + b8
            h_hi = a8 * pltpu.roll(h_lo, p, 0) + b8
            a_ref[rows, :] = jnp.where(upper, h_hi, h_lo)
            return jnp.where(upper, h_hi, pltpu.roll(h_hi, p, 0))

        hc_ref[...] = lax.fori_loop(0, bt // SUBLANES, scan, hc_ref[...], unroll=2)

        def gate_out(i, carry):
            rows = pl.ds(pl.multiple_of(i * rg, rg), rg)
            y_ref[rows, :] = (gg_ref[rows, :].astype(F32) * a_ref[rows, :]).astype(y_ref.dtype)
            return carry

        lax.fori_loop(0, bt // rg, gate_out, 0)

    @pl.when(t == n_t - 1)
    def _():
        conv_ref[...] = s_ref[0:halo, :]
        hlast_ref[...] = hc_ref[...]


def _rglru(xb, gg, hist, h0, wconv, bconv, wra, bra, wix, bix, lam, y_prev, *, row0, nrows, p, bt,
           block, blocks_per_group, win):
    m, c = xb.shape
    gw = block * blocks_per_group
    n_groups = c // gw
    halo = hist.shape[0]
    hrows = h0.shape[0]
    starts = _window_starts(block, blocks_per_group, gw, win)
    rb = row0 // bt
    aliased = y_prev is not None
    n_t = nrows // bt
    n_zero = 0 if aliased else (m - row0 - nrows) // bt

    row_blk = lambda g, t: (rb + t, g)
    in_blk = lambda g, t: (rb + jnp.minimum(t, n_t - 1), g)
    vec = pl.BlockSpec((1, gw), lambda g, t: (0, g))
    wspec = pl.BlockSpec((blocks_per_group, win, win), lambda g, t: (g, 0, 0))
    in_specs = [
        pl.BlockSpec((bt, gw), in_blk),
        pl.BlockSpec((bt, gw), in_blk),
        pl.BlockSpec((halo, gw), lambda g, t: (0, g)),
        pl.BlockSpec((hrows, gw), lambda g, t: (0, g)),
        pl.BlockSpec((CONV_W, gw), lambda g, t: (0, g)),
        vec, wspec, vec, wspec, vec, vec,
    ]
    args = [xb, gg, hist, h0, wconv, bconv.reshape(1, c), wra, bra.reshape(1, c), wix, bix.reshape(1, c),
            lam.reshape(1, c)]
    aliases = {}
    if aliased:
        in_specs = [pl.BlockSpec(memory_space=pl.ANY)] + in_specs
        args = [y_prev] + args
        aliases = {0: 0}
    return pl.pallas_call(
        functools.partial(_rglru_kernel, p=p, halo=halo, starts=starts, win=win, aliased=aliased, n_t=n_t),
        grid=(n_groups, n_t + n_zero),
        in_specs=in_specs,
        out_specs=[
            pl.BlockSpec((bt, gw), row_blk),
            pl.BlockSpec((halo, gw), lambda g, t: (0, g)),
            pl.BlockSpec((hrows, gw), lambda g, t: (0, g)),
        ],
        out_shape=[
            jax.ShapeDtypeStruct((m, c), BF16),
            jax.ShapeDtypeStruct((halo, c), F32),
            jax.ShapeDtypeStruct((hrows, c), F32),
        ],
        scratch_shapes=[
            pltpu.VMEM((halo + bt, gw), F32),
            pltpu.VMEM((bt, gw), F32),
            pltpu.VMEM((bt, gw), BF16),
            pltpu.VMEM((bt, gw), F32),
            pltpu.VMEM((bt, gw), F32),
            pltpu.VMEM((hrows, gw), F32),
        ],
        input_output_aliases=aliases,
        compiler_params=_params("arbitrary", "arbitrary"),
        name="rglru",
    )(*args)


def _pad_block_weights(w, block, blocks_per_group, win):
    gw = block * blocks_per_group
    starts = _window_starts(block, blocks_per_group, gw, win)
    n_groups = w.shape[0] // blocks_per_group
    wg = w.astype(BF16).reshape(n_groups, blocks_per_group, block, block)
    padded = []
    for bl, lo in enumerate(starts):
        off = bl * block - lo
        rest = win - off - block
        padded.append(jnp.pad(wg[:, bl], ((0, 0), (off, rest), (off, rest))))
    return jnp.stack(padded, axis=1).reshape(w.shape[0], win, win)


def kernel(x_prompt, x_sample, state_conv, state_h, c_prompt, c_sample, w_ada, b_ada, g_norm1, g_norm2, g_final,
           w_in_a, b_in_a, g_v_a, b_v_a, w_s_a, b_s_a, w_out_a, w_in_b, b_in_b, w_conv_b, b_conv_b, w_ra_b, b_ra_b,
           w_ix_b, b_ix_b, lam_b, w_out_b, w_ff1, w_ff3, w_ff2):
    nb, t_p, d = x_prompt.shape
    ns, t_s, _ = x_sample.shape
    depth = w_ada.shape[0]
    d_a = w_out_a.shape[1]
    d_rnn = w_out_b.shape[1]
    group = d_a // N_GROUPS_A
    block = d_rnn // N_BLOCKS_B
    mp, ms = nb * t_p, ns * t_s
    assert ns == PAT and PAT % nb == 0 and t_p % CHUNK == 0 and t_s <= CHUNK and nb == 4

    x = _to_time_major(x_prompt, x_sample.transpose(1, 0, 2).reshape(ms, d))
    c_pat = jnp.concatenate([jnp.tile(c_prompt, (PAT // nb, 1)), c_sample])
    mod = _ada(c_pat, w_ada, b_ada)


    blocks_per_group = LANES // math.gcd(block, LANES)
    win = (-(-block // LANES) + 1) * LANES
    halo_p = -(-(CONV_W - 1) * nb // SUBLANES) * SUBLANES

    v_new, conv_p, h_p, conv_s, h_s = [], [], [], [], []
    for i in range(depth):
        j = i // 2
        h = _rms_mod(x, g_norm1[i], mod, i, 0, 1, mp)
        if i % 2 == 0:
            u = _mm_bias_act(h, w_in_a, b_in_a, j, 0, d_a, "gelu", BF16)
            v = _mm_bias_act(h, w_in_a, b_in_a, j, d_a, d_a, "gelu", F32)
            vn_s = _layernorm(v, g_v_a[j], b_v_a[j], F32, mp, ms)
            bias_t = jnp.repeat(b_s_a[j], nb, axis=1).T
            y = _gate_prompt(u, v, g_v_a[j], b_v_a[j], w_s_a[j], bias_t, mp, group, nb)
            w_small = jnp.where(jnp.tril(jnp.ones((t_s, t_s), bool)), w_s_a[j][:, :t_s, :t_s], 0.0)
            y = _gate_sample(w_small.reshape(N_GROUPS_A, t_s * t_s), b_s_a[j][:, :t_s], u, vn_s, y, mp, t_s, ns, group)
            v_new.append(vn_s.reshape(t_s, ns, d_a).transpose(1, 0, 2))
            x = _mm_resid(y, w_out_a, j, x, mod, i, 2, mp, 1024, 512, order="nm")
        else:
            gg, w_out_bf = _mm_bias_act(h, w_in_b, b_in_b, j, 0, d_rnn, "gelu", BF16, side=w_out_b)
            xb = _mm_bias_act(h, w_in_b, b_in_b, j, d_rnn, d_rnn, None, F32)
            wra = _pad_block_weights(0.5 * w_ra_b[j], block, blocks_per_group, win)
            wix = _pad_block_weights(0.5 * w_ix_b[j], block, blocks_per_group, win)
            common = (w_conv_b[j], b_conv_b[j], wra, b_ra_b[j], wix, b_ix_b[j], lam_b[j])
            geom = dict(block=block, blocks_per_group=blocks_per_group, win=win)
            y, cp, hp = _rglru(xb, gg, jnp.zeros((halo_p, d_rnn), F32), jnp.zeros((SUBLANES, d_rnn), F32), *common,
                               None, row0=0, nrows=mp, p=nb, bt=512, **geom)
            hist_s = state_conv[j].transpose(1, 0, 2).reshape((CONV_W - 1) * ns, d_rnn)
            y, cs, hs = _rglru(xb, gg, hist_s, state_h[j], *common, y, row0=mp, nrows=ms, p=ns, bt=2 * ns, **geom)
            conv_p.append(cp[halo_p - (CONV_W - 1) * nb:].reshape(CONV_W - 1, nb, d_rnn).transpose(1, 0, 2))
            h_p.append(hp[SUBLANES - nb:])
            conv_s.append(cs.reshape(CONV_W - 1, ns, d_rnn).transpose(1, 0, 2))
            h_s.append(hs)
            x = _mm_resid(y, w_out_bf[None], 0, x, mod, i, 2, mp, 1024, 512, order="nm")
        h = _rms_mod(x, g_norm2[i], mod, i, 3, 4, mp)
        f, w2b = _mm_swiglu(h, w_ff1, w_ff3, w_ff2, i)
        x = _mm_resid(f, w2b[None], 0, x, mod, i, 5, mp, 512, 512, order="nm")

    y_prompt = _rms_seq_major(x, g_final, nb, t_p)
    y_sample = _rms(x, g_final, mp, ms).reshape(t_s, ns, d).transpose(1, 0, 2)
    return (y_prompt, y_sample, jnp.stack(v_new), jnp.stack(conv_p), jnp.stack(h_p), jnp.stack(conv_s),
            jnp.stack(h_s))
```

```python
import functools
import math

import jax
import jax.numpy as jnp
from jax import lax
from jax.experimental import pallas as pl
from jax.experimental.pallas import tpu as pltpu

EPS = 1e-6
LRU_C = 8.0
CHUNK = 128
N_GROUPS_A = 16
N_BLOCKS_B = 16
CONV_W = 4

LANES = 128
SUBLANES = 8
PAT = 128
VMEM_LIMIT_BYTES = 56 * 1024 * 1024

F32 = jnp.float32
BF16 = jnp.bfloat16


def _params(*sem):
    return pltpu.CompilerParams(dimension_semantics=sem, vmem_limit_bytes=VMEM_LIMIT_BYTES)


def _sigmoid(x):
    return 0.5 * (jnp.tanh(0.5 * x) + 1.0)


def _ada_kernel(c_ref, w_ref, b_ref, o_ref, act_ref):
    @pl.when((pl.program_id(0) == 0) & (pl.program_id(1) == 0))
    def _():
        c = c_ref[...]
        act_ref[...] = (c * _sigmoid(c)).astype(BF16)

    w = w_ref[...].astype(BF16)
    o_ref[...] = jnp.dot(act_ref[...], w, preferred_element_type=F32) + b_ref[...]


def _ada(c_pat, w_ada, b_ada, bn=1024):
    depth, d, n6 = w_ada.shape
    rows = c_pat.shape[0]
    return pl.pallas_call(
        _ada_kernel,
        grid=(depth, n6 // bn),
        in_specs=[
            pl.BlockSpec((rows, d), lambda i, n: (0, 0)),
            pl.BlockSpec((None, d, bn), lambda i, n: (i, 0, n)),
            pl.BlockSpec((None, 1, bn), lambda i, n: (i, 0, n)),
        ],
        out_specs=pl.BlockSpec((None, rows, bn), lambda i, n: (i, 0, n)),
        out_shape=jax.ShapeDtypeStruct((depth, rows, n6), F32),
        scratch_shapes=[pltpu.VMEM((rows, d), BF16)],
        compiler_params=_params("arbitrary", "arbitrary"),
        name="ada",
    )(c_pat, w_ada, b_ada.reshape(depth, 1, n6))


COL_CHUNK = 4 * LANES


def _row_sum(ref, rows, fn):
    acc = None
    for c in range(ref.shape[1] // LANES):
        t = fn(ref[rows, c * LANES:(c + 1) * LANES])
        acc = t if acc is None else acc + t
    return jnp.sum(acc, axis=-1, keepdims=True)


def _col_chunks(d):
    return [slice(c, min(c + COL_CHUNK, d)) for c in range(0, d, COL_CHUNK)]


def _rms_mod_kernel(x_ref, g_ref, sh_ref, sc_ref, o_ref, gain_ref):
    d = x_ref.shape[1]
    gain_ref[...] = g_ref[...] * (1.0 + sc_ref[...])
    for k in range(x_ref.shape[0] // PAT):
        rows = slice(k * PAT, (k + 1) * PAT)
        rstd = lax.rsqrt(_row_sum(x_ref, rows, lambda t: t * t) * (1.0 / d) + EPS)
        for cols in _col_chunks(d):
            o_ref[rows, cols] = ((x_ref[rows, cols] * rstd) * gain_ref[:, cols] + sh_ref[:, cols]).astype(o_ref.dtype)


def _rms_mod(x, g, mod, layer, piece_shift, piece_scale, mp, bm=512):
    m, d = x.shape
    n_prompt_tiles = mp // bm

    def pat(piece):
        return lambda i: (layer, jnp.where(i >= n_prompt_tiles, 1, 0), piece)

    return pl.pallas_call(
        _rms_mod_kernel,
        grid=(m // bm,),
        in_specs=[
            pl.BlockSpec((bm, d), lambda i: (i, 0)),
            pl.BlockSpec((1, d), lambda i: (0, 0)),
            pl.BlockSpec((None, PAT, d), pat(piece_shift)),
            pl.BlockSpec((None, PAT, d), pat(piece_scale)),
        ],
        out_specs=pl.BlockSpec((bm, d), lambda i: (i, 0)),
        out_shape=jax.ShapeDtypeStruct((m, d), BF16),
        scratch_shapes=[pltpu.VMEM((PAT, d), F32)],
        compiler_params=_params("arbitrary"),
        name="rms_mod",
    )(x, g.reshape(1, d), mod, mod)


def _rms_kernel(x_ref, g_ref, o_ref):
    x = x_ref[...]
    o_ref[...] = x * lax.rsqrt(jnp.mean(x * x, axis=-1, keepdims=True) + EPS) * g_ref[...]


def _rms(x, g, row0, nrows, bm=256):
    d = x.shape[1]
    off = row0 // bm
    return pl.pallas_call(
        _rms_kernel,
        grid=(nrows // bm,),
        in_specs=[pl.BlockSpec((bm, d), lambda i: (i + off, 0)), pl.BlockSpec((1, d), lambda i: (0, 0))],
        out_specs=pl.BlockSpec((bm, d), lambda i: (i, 0)),
        out_shape=jax.ShapeDtypeStruct((nrows, d), F32),
        compiler_params=_params("arbitrary"),
        name="rms_final",
    )(x, g.reshape(1, d))


def _rms_seq_major_kernel(x_ref, g_ref, o_ref, s_ref):
    nb, bt, d = o_ref.shape

    for k in range(x_ref.shape[0] // PAT):
        rows = slice(k * PAT, (k + 1) * PAT)
        rstd = lax.rsqrt(_row_sum(x_ref, rows, lambda t: t * t) * (1.0 / d) + EPS)
        for c in range(d // LANES):
            cols = slice(c * LANES, (c + 1) * LANES)
            s_ref[c, rows, :] = x_ref[rows, cols] * rstd * g_ref[:, cols]
    for c in range(d // LANES):
        for s in range(nb):
            o_ref[s, :, c * LANES:(c + 1) * LANES] = s_ref[c, pl.ds(s, bt, stride=nb), :]


def _rms_seq_major(x, g, nb, t_len, bt=128):
    d = x.shape[1]
    return pl.pallas_call(
        _rms_seq_major_kernel,
        grid=(t_len // bt,),
        in_specs=[pl.BlockSpec((bt * nb, d), lambda i: (i, 0)), pl.BlockSpec((1, d), lambda i: (0, 0))],
        out_specs=pl.BlockSpec((nb, bt, d), lambda i: (0, i, 0)),
        out_shape=jax.ShapeDtypeStruct((nb, t_len, d), F32),
        scratch_shapes=[pltpu.VMEM((d // LANES, bt * nb, LANES), F32)],
        compiler_params=_params("arbitrary"),
        name="rms_final_prompt",
    )(x, g.reshape(1, d))


def _to_time_major_kernel(xp_ref, xs_ref, o_ref, s_ref, *, n_prompt_tiles):
    nb, bt, d = xp_ref.shape
    i = pl.program_id(0)

    @pl.when(i < n_prompt_tiles)
    def _():
        for c in range(d // LANES):
            for s in range(nb):
                s_ref[c, pl.ds(s, bt, stride=nb), :] = xp_ref[s, :, c * LANES:(c + 1) * LANES]
        for c in range(d // LANES):
            o_ref[:, c * LANES:(c + 1) * LANES] = s_ref[c]

    @pl.when(i >= n_prompt_tiles)
    def _():
        o_ref[...] = xs_ref[...]


def _to_time_major(x_prompt, xs_tm, bt=128):
    nb, t_len, d = x_prompt.shape
    ms = xs_tm.shape[0]
    rows = bt * nb
    n_prompt_tiles = t_len // bt
    return pl.pallas_call(
        functools.partial(_to_time_major_kernel, n_prompt_tiles=n_prompt_tiles),
        grid=(n_prompt_tiles + ms // rows,),
        in_specs=[
            pl.BlockSpec((nb, bt, d), lambda i: (0, jnp.minimum(i, n_prompt_tiles - 1), 0)),
            pl.BlockSpec((rows, d), lambda i: (jnp.maximum(i - n_prompt_tiles, 0), 0)),
        ],
        out_specs=pl.BlockSpec((rows, d), lambda i: (i, 0)),
        out_shape=jax.ShapeDtypeStruct((nb * t_len + ms, d), F32),
        scratch_shapes=[pltpu.VMEM((d // LANES, rows, LANES), F32)],
        compiler_params=_params("arbitrary"),
        name="to_time_major",
    )(x_prompt, xs_tm)


def _layernorm_kernel(x_ref, g_ref, b_ref, o_ref):
    x = x_ref[...]
    mu = jnp.mean(x, axis=-1, keepdims=True)
    xc = x - mu
    var = jnp.mean(xc * xc, axis=-1, keepdims=True)
    o_ref[...] = (xc * lax.rsqrt(var + EPS) * g_ref[...] + b_ref[...]).astype(o_ref.dtype)


def _layernorm(x, g, b, out_dtype, row0, nrows, bm=256):
    d = x.shape[1]
    off = row0 // bm
    return pl.pallas_call(
        _layernorm_kernel,
        grid=(nrows // bm,),
        in_specs=[
            pl.BlockSpec((bm, d), lambda i: (i + off, 0)),
            pl.BlockSpec((1, d), lambda i: (0, 0)),
            pl.BlockSpec((1, d), lambda i: (0, 0)),
        ],
        out_specs=pl.BlockSpec((bm, d), lambda i: (i, 0)),
        out_shape=jax.ShapeDtypeStruct((nrows, d), out_dtype),
        compiler_params=_params("arbitrary"),
        name="layernorm",
    )(x, g.reshape(1, d), b.reshape(1, d))


def _grid_order(order, n_i, n_j):
    if order == "mn":
        return (n_i, n_j), (lambda f: f)
    return (n_j, n_i), (lambda f: (lambda a, b: f(b, a)))


def _mm_bias_act_kernel(x_ref, w_ref, b_ref, *rest, act, side_chunks):
    o_ref = rest[-2] if side_chunks else rest[0]
    acc = jnp.dot(x_ref[...], w_ref[...].astype(BF16), preferred_element_type=F32) + b_ref[...]
    if act == "gelu":
        acc = jax.nn.gelu(acc)
    o_ref[...] = acc.astype(o_ref.dtype)

    if side_chunks:
        side_ref, _, side_out_ref = rest
        step = pl.program_id(0) * pl.num_programs(1) + pl.program_id(1)

        @pl.when(step < side_chunks)
        def _():
            side_out_ref[...] = side_ref[...].astype(BF16)


def _mm_bias_act(x, w, b, layer, col0, ncols, act, out_dtype, side=None, bm=1024, bn=512):
    m, k = x.shape
    noff = col0 // bn
    nj = ncols // bn
    b3 = b.reshape(b.shape[0], 1, b.shape[1])
    in_specs = [
        pl.BlockSpec((bm, k), lambda i, n: (i, 0)),
        pl.BlockSpec((None, k, bn), lambda i, n: (layer, 0, n + noff)),
        pl.BlockSpec((None, 1, bn), lambda i, n: (layer, 0, n + noff)),
    ]
    out_specs = [pl.BlockSpec((bm, bn), lambda i, n: (i, n))]
    out_shape = [jax.ShapeDtypeStruct((m, ncols), out_dtype)]
    args = [x, w, b3]
    side_chunks = 0
    if side is not None:
        _, sk, sn = side.shape
        side_rows = next(r for r in (128, 256, 512, 1024) if sk % r == 0 and sk // r <= (m // bm) * nj)
        side_chunks = sk // side_rows
        chunk = lambda i, n: jnp.minimum(i * nj + n, side_chunks - 1)
        in_specs.append(pl.BlockSpec((None, side_rows, sn), lambda i, n: (layer, chunk(i, n), 0)))
        out_specs.append(pl.BlockSpec((side_rows, sn), lambda i, n: (chunk(i, n), 0)))
        out_shape.append(jax.ShapeDtypeStruct((sk, sn), BF16))
        args.append(side)
    out = pl.pallas_call(
        functools.partial(_mm_bias_act_kernel, act=act, side_chunks=side_chunks),
        grid=(m // bm, nj),
        in_specs=in_specs,
        out_specs=out_specs,
        out_shape=out_shape,
        compiler_params=_params("arbitrary", "arbitrary"),
        name="mm_bias_act",
    )(*args)
    return out if side is not None else out[0]


def _mm_swiglu_kernel(x_ref, w1_ref, w3_ref, w2_ref, o_ref, w2b_ref, *, side_tiles):
    x = x_ref[...]
    a = jnp.dot(x, w1_ref[...].astype(BF16), preferred_element_type=F32)
    b = jnp.dot(x, w3_ref[...].astype(BF16), preferred_element_type=F32)
    o_ref[...] = (a * _sigmoid(a) * b).astype(o_ref.dtype)

    @pl.when(pl.program_id(0) < side_tiles)
    def _():
        w2b_ref[...] = w2_ref[...].astype(BF16)


def _mm_swiglu(x, w1, w3, w2, layer, bm=1024, bn=256):
    m, k = x.shape
    n = w1.shape[2]
    nj = n // bn
    side_tiles = min(2, m // bm)
    n_chunks = side_tiles * nj
    rk, n2 = w2.shape[1] // n_chunks, w2.shape[2]
    assert rk * n_chunks == w2.shape[1] and rk % (2 * SUBLANES) == 0
    wspec = pl.BlockSpec((None, k, bn), lambda i, j: (layer, 0, j))
    chunk = lambda i, j: jnp.where(i < side_tiles, i * nj + j, n_chunks - 1)
    return pl.pallas_call(
        functools.partial(_mm_swiglu_kernel, side_tiles=side_tiles),
        grid=(m // bm, nj),
        in_specs=[
            pl.BlockSpec((bm, k), lambda i, j: (i, 0)), wspec, wspec,
            pl.BlockSpec((None, rk, n2), lambda i, j: (layer, chunk(i, j), 0)),
        ],
        out_specs=[
            pl.BlockSpec((bm, bn), lambda i, j: (i, j)),
            pl.BlockSpec((rk, n2), lambda i, j: (chunk(i, j), 0)),
        ],
        out_shape=[jax.ShapeDtypeStruct((m, n), BF16), jax.ShapeDtypeStruct((n_chunks * rk, n2), BF16)],
        compiler_params=_params("arbitrary", "arbitrary"),
        name="mm_swiglu",
    )(x, w1, w3, w2)


def _mm_resid_kernel(x_ref, w_ref, res_ref, gate_ref, o_ref):
    acc = jnp.dot(x_ref[...], w_ref[...].astype(BF16), preferred_element_type=F32)
    gate = gate_ref[...]
    for k in range(acc.shape[0] // PAT):
        rows = slice(k * PAT, (k + 1) * PAT)
        o_ref[rows, :] = res_ref[rows, :] + gate * acc[rows, :]


def _mm_resid(x, w, layer_w, res, mod, layer, piece_gate, mp, bm, bn, order="mn"):
    m, k = x.shape
    n = w.shape[2]
    n_prompt_tiles = mp // bm
    gate_col0 = piece_gate * (n // bn)
    grid, ix = _grid_order(order, m // bm, n // bn)
    return pl.pallas_call(
        _mm_resid_kernel,
        grid=grid,
        in_specs=[
            pl.BlockSpec((bm, k), ix(lambda i, j: (i, 0))),
            pl.BlockSpec((None, k, bn), ix(lambda i, j: (layer_w, 0, j))),
            pl.BlockSpec((bm, bn), ix(lambda i, j: (i, j))),
            pl.BlockSpec((None, PAT, bn),
                         ix(lambda i, j: (layer, jnp.where(i >= n_prompt_tiles, 1, 0), gate_col0 + j))),
        ],
        out_specs=pl.BlockSpec((bm, bn), ix(lambda i, j: (i, j))),
        out_shape=jax.ShapeDtypeStruct((m, n), F32),
        input_output_aliases={2: 0},
        compiler_params=_params("arbitrary", "arbitrary"),
        name="mm_resid",
    )(x, w, res, mod)


def _gate_prompt_kernel(u_ref, v_ref, gv_ref, bv_ref, w_ref, bias_ref, y_ref, mix_ref, vn_ref, *, group, nb, n_chunks):
    c = pl.program_id(0)
    gpt, r, _ = mix_ref.shape
    chunk = w_ref.shape[1]

    @pl.when(c == 0)
    def _():
        shift = nb.bit_length() - 1
        rep = (lax.shift_right_logical(lax.broadcasted_iota(jnp.int32, (r, chunk), 0), shift)
               == lax.broadcasted_iota(jnp.int32, (r, chunk), 1)).astype(BF16)
        causal = lax.broadcasted_iota(jnp.int32, (chunk, chunk), 0) >= lax.broadcasted_iota(jnp.int32, (chunk, chunk), 1)
        same_seq = ((lax.broadcasted_iota(jnp.int32, (r, r), 0) & (nb - 1))
                    == (lax.broadcasted_iota(jnp.int32, (r, r), 1) & (nb - 1)))
        for g in range(gpt):
            wg = jnp.where(causal, w_ref[g], 0.0).astype(BF16)
            left = jnp.dot(rep, wg, preferred_element_type=F32).astype(BF16)
            full = lax.dot_general(left, rep, (((1,), (1,)), ((), ())), preferred_element_type=F32)
            mix_ref[g] = jnp.where(same_seq, full, 0.0).astype(BF16)

    @pl.when(c < n_chunks)
    def _():
        d = v_ref.shape[1]
        for k in range(r // PAT):
            rows = slice(k * PAT, (k + 1) * PAT)
            mu = _row_sum(v_ref, rows, lambda t: t) * (1.0 / d)
            var = _row_sum(v_ref, rows, lambda t: (t - mu) * (t - mu)) * (1.0 / d)
            rstd = lax.rsqrt(var + EPS)
            for cols in _col_chunks(d):
                vn_ref[rows, cols] = ((v_ref[rows, cols] - mu) * rstd * gv_ref[:, cols] + bv_ref[:, cols]).astype(BF16)
        for g in range(gpt):
            cols = slice(g * group, (g + 1) * group)
            sv = jnp.dot(mix_ref[g], vn_ref[:, cols], preferred_element_type=F32) + bias_ref[:, g:g + 1]
            y_ref[:, cols] = (u_ref[:, cols].astype(F32) * sv).astype(y_ref.dtype)

    @pl.when(c >= n_chunks)
    def _():
        y_ref[...] = jnp.zeros(y_ref.shape, y_ref.dtype)


def _gate_prompt(u, v, g_v, b_v, w_s, bias_t, mp, group, nb):
    m, d = u.shape
    n_groups, chunk, _ = w_s.shape
    r = chunk * nb
    n_chunks = mp // r
    row_blk = lambda c: (jnp.minimum(c, n_chunks - 1), 0)
    whole = lambda shape: pl.BlockSpec(shape, lambda c: (0,) * len(shape))
    return pl.pallas_call(
        functools.partial(_gate_prompt_kernel, group=group, nb=nb, n_chunks=n_chunks),
        grid=(m // r,),
        in_specs=[
            pl.BlockSpec((r, d), row_blk),
            pl.BlockSpec((r, d), row_blk),
            whole((1, d)), whole((1, d)), whole((n_groups, chunk, chunk)), whole((r, n_groups)),
        ],
        out_specs=pl.BlockSpec((r, d), lambda c: (c, 0)),
        out_shape=jax.ShapeDtypeStruct((m, d), BF16),
        scratch_shapes=[pltpu.VMEM((n_groups, r, r), BF16), pltpu.VMEM((r, d), BF16)],
        compiler_params=_params("arbitrary"),
        name="gate_prompt",
    )(u, v, g_v.reshape(1, d), b_v.reshape(1, d), w_s, bias_t)


def _gate_sample_kernel(w_ref, b_ref, u_ref, v_ref, y_any_ref, y_ref, *, steps, p):
    del y_any_ref
    g = pl.program_id(0)
    for t in range(steps):
        acc = w_ref[g, t * steps] * v_ref[0:p, :]
        for s in range(1, t + 1):
            acc = acc + w_ref[g, t * steps + s] * v_ref[s * p:(s + 1) * p, :]
        acc = acc + b_ref[g, t]
        rows = slice(t * p, (t + 1) * p)
        y_ref[rows, :] = (u_ref[rows, :].astype(F32) * acc).astype(y_ref.dtype)


def _gate_sample(w_small, b_small, u, vn_s, y, mp, steps, p, group):
    ms = steps * p
    n_groups = w_small.shape[0]
    row_blk = mp // ms
    return pl.pallas_call(
        functools.partial(_gate_sample_kernel, steps=steps, p=p),
        grid=(n_groups,),
        in_specs=[
            pl.BlockSpec(memory_space=pltpu.SMEM),
            pl.BlockSpec(memory_space=pltpu.SMEM),
            pl.BlockSpec((ms, group), lambda g: (row_blk, g)),
            pl.BlockSpec((ms, group), lambda g: (0, g)),
            pl.BlockSpec(memory_space=pl.ANY),
        ],
        out_specs=pl.BlockSpec((ms, group), lambda g: (row_blk, g)),
        out_shape=jax.ShapeDtypeStruct(y.shape, y.dtype),
        input_output_aliases={4: 0},
        compiler_params=_params("arbitrary"),
        name="gate_sample",
    )(w_small, b_small, u, vn_s, y)


def _window_starts(block, blocks_per_group, group_w, win):
    return [min((bl * block) // LANES * LANES, group_w - win) for bl in range(blocks_per_group)]


def _rglru_kernel(*refs, n_t, **static):
    y_ref = refs[-(_RGLRU_N_OUT + _RGLRU_N_SCRATCH)]
    t = pl.program_id(1)
    pl.when(t < n_t)(functools.partial(_rglru_tile, *refs, n_t=n_t, **static))

    @pl.when(t >= n_t)
    def _():
        y_ref[...] = jnp.zeros(y_ref.shape, y_ref.dtype)


_RGLRU_N_OUT = 3
_RGLRU_N_SCRATCH = 6


def _rglru_tile(*refs, p, halo, starts, win, aliased, n_t):
    if aliased:
        refs = refs[1:]
    (xb_ref, gg_ref, hist_ref, h0_ref, wc_ref, bc_ref, wra_ref, bra_ref, wix_ref, bix_ref, lam_ref,
     y_ref, conv_ref, hlast_ref, s_ref, xc_ref, xcb_ref, a_ref, bx_ref, hc_ref) = refs
    t = pl.program_id(1)
    bt, gw = xb_ref.shape

    @pl.when(t == 0)
    def _():
        s_ref[0:halo, :] = hist_ref[...]
        hc_ref[...] = h0_ref[...]

    s_ref[halo:halo + bt, :] = xb_ref[...]
    taps = [halo - (CONV_W - 1 - k) * p for k in range(CONV_W)]
    rc = min(bt, 64)

    def conv(r0, cols):
        sl = s_ref[r0:r0 + rc + halo, cols]
        rl = sl if p % SUBLANES == 0 else pltpu.roll(sl, p, 0)
        xc = bc_ref[:, cols]
        for k, lo in enumerate(taps):
            src = sl[lo:lo + rc] if lo % SUBLANES == 0 else rl[lo + p:lo + p + rc]
            xc = xc + wc_ref[k:k + 1, cols] * src
        xc_ref[r0:r0 + rc, cols] = xc
        xcb_ref[r0:r0 + rc, cols] = xc.astype(BF16)

    lam = lam_ref[...]
    c8h = (-0.5 * LRU_C) * (jnp.maximum(-lam, 0.0) + jnp.log1p(jnp.exp(-jnp.abs(lam))))

    rg = min(bt, 32)

    def gates(rows, cols):
        c = c8h[:, cols]
        log_a = c * jnp.tanh(a_ref[rows, cols]) + c
        a = jnp.exp(log_a)
        mult = jnp.sqrt(jnp.tanh(log_a) * (-1.0 - a * a))
        a_ref[rows, cols] = a
        bx_ref[rows, cols] = mult * (xc_ref[rows, cols] * (0.5 * jnp.tanh(bx_ref[rows, cols]) + 0.5))

    a_ref[...] = jnp.broadcast_to(0.5 * bra_ref[...], (bt, gw))
    bx_ref[...] = jnp.broadcast_to(0.5 * bix_ref[...], (bt, gw))
    col0 = conv_done = 0
    for bl, lo in enumerate(starts):
        if lo + win > conv_done:
            for r in range(bt // rc):
                conv(r * rc, slice(conv_done, lo + win))
            conv_done = lo + win
        xw = xcb_ref[:, lo:lo + win]
        a_ref[:, lo:lo + win] += jnp.dot(xw, wra_ref[bl], preferred_element_type=F32)
        bx_ref[:, lo:lo + win] += jnp.dot(xw, wix_ref[bl], preferred_element_type=F32)
        col1 = starts[bl + 1] if bl + 1 < len(starts) else gw
        for r in range(bt // rg):
            gates(slice(r * rg, (r + 1) * rg), slice(col0, col1))
        col0 = col1
    s_ref[0:halo, :] = s_ref[bt:bt + halo, :]

    if p % SUBLANES == 0:
        h = hc_ref[...]
        for step in range(bt // p):
            rows = slice(step * p, (step + 1) * p)
            h = a_ref[rows, :] * h + bx_ref[rows, :]
            y_ref[rows, :] = (gg_ref[rows, :].astype(F32) * h).astype(y_ref.dtype)
        hc_ref[...] = h
    else:
        upper = lax.broadcasted_iota(jnp.int32, (SUBLANES, gw), 0) >= p

        def scan(i, hc):
            rows = pl.ds(pl.multiple_of(i * SUBLANES, SUBLANES), SUBLANES)
            a8 = a_ref[rows, :]
            b8 = bx_ref[rows, :]
            h_lo = a8 * hc + b8
            h_hi = a8 * pltpu.roll(h_lo, p, 0) + b8
            a_ref[rows, :] = jnp.where(upper, h_hi, h_lo)
            return jnp.where(upper, h_hi, pltpu.roll(h_hi, p, 0))

        hc_ref[...] = lax.fori_loop(0, bt // SUBLANES, scan, hc_ref[...], unroll=2)

        def gate_out(i, carry):
            rows = pl.ds(pl.multiple_of(i * rg, rg), rg)
            y_ref[rows, :] = (gg_ref[rows, :].astype(F32) * a_ref[rows, :]).astype(y_ref.dtype)
            return carry

        lax.fori_loop(0, bt // rg, gate_out, 0)

    @pl.when(t == n_t - 1)
    def _():
        conv_ref[...] = s_ref[0:halo, :]
        hlast_ref[...] = hc_ref[...]


def _rglru(xb, gg, hist, h0, wconv, bconv, wra, bra, wix, bix, lam, y_prev, *, row0, nrows, p, bt,
           block, blocks_per_group, win):
    m, c = xb.shape
    gw = block * blocks_per_group
    n_groups = c // gw
    halo = hist.shape[0]
    hrows = h0.shape[0]
    starts = _window_starts(block, blocks_per_group, gw, win)
    rb = row0 // bt
    aliased = y_prev is not None
    n_t = nrows // bt
    n_zero = 0 if aliased else (m - row0 - nrows) // bt

    row_blk = lambda g, t: (rb + t, g)
    in_blk = lambda g, t: (rb + jnp.minimum(t, n_t - 1), g)
    vec = pl.BlockSpec((1, gw), lambda g, t: (0, g))
    wspec = pl.BlockSpec((blocks_per_group, win, win), lambda g, t: (g, 0, 0))
    in_specs = [
        pl.BlockSpec((bt, gw), in_blk),
        pl.BlockSpec((bt, gw), in_blk),
        pl.BlockSpec((halo, gw), lambda g, t: (0, g)),
        pl.BlockSpec((hrows, gw), lambda g, t: (0, g)),
        pl.BlockSpec((CONV_W, gw), lambda g, t: (0, g)),
        vec, wspec, vec, wspec, vec, vec,
    ]
    args = [xb, gg, hist, h0, wconv, bconv.reshape(1, c), wra, bra.reshape(1, c), wix, bix.reshape(1, c),
            lam.reshape(1, c)]
    aliases = {}
    if aliased:
        in_specs = [pl.BlockSpec(memory_space=pl.ANY)] + in_specs
        args = [y_prev] + args
        aliases = {0: 0}
    return pl.pallas_call(
        functools.partial(_rglru_kernel, p=p, halo=halo, starts=starts, win=win, aliased=aliased, n_t=n_t),
        grid=(n_groups, n_t + n_zero),
        in_specs=in_specs,
        out_specs=[
            pl.BlockSpec((bt, gw), row_blk),
            pl.BlockSpec((halo, gw), lambda g, t: (0, g)),
            pl.BlockSpec((hrows, gw), lambda g, t: (0, g)),
        ],
        out_shape=[
            jax.ShapeDtypeStruct((m, c), BF16),
            jax.ShapeDtypeStruct((halo, c), F32),
            jax.ShapeDtypeStruct((hrows, c), F32),
        ],
        scratch_shapes=[
            pltpu.VMEM((halo + bt, gw), F32),
            pltpu.VMEM((bt, gw), F32),
            pltpu.VMEM((bt, gw), BF16),
            pltpu.VMEM((bt, gw), F32),
            pltpu.VMEM((bt, gw), F32),
            pltpu.VMEM((hrows, gw), F32),
        ],
        input_output_aliases=aliases,
        compiler_params=_params("arbitrary", "arbitrary"),
        name="rglru",
    )(*args)


def _pad_block_weights(w, block, blocks_per_group, win):
    gw = block * blocks_per_group
    starts = _window_starts(block, blocks_per_group, gw, win)
    n_groups = w.shape[0] // blocks_per_group
    wg = w.astype(BF16).reshape(n_groups, blocks_per_group, block, block)
    padded = []
    for bl, lo in enumerate(starts):
        off = bl * block - lo
        rest = win - off - block
        padded.append(jnp.pad(wg[:, bl], ((0, 0), (off, rest), (off, rest))))
    return jnp.stack(padded, axis=1).reshape(w.shape[0], win, win)


def kernel(x_prompt, x_sample, state_conv, state_h, c_prompt, c_sample, w_ada, b_ada, g_norm1, g_norm2, g_final,
           w_in_a, b_in_a, g_v_a, b_v_a, w_s_a, b_s_a, w_out_a, w_in_b, b_in_b, w_conv_b, b_conv_b, w_ra_b, b_ra_b,
           w_ix_b, b_ix_b, lam_b, w_out_b, w_ff1, w_ff3, w_ff2):
    nb, t_p, d = x_prompt.shape
    ns, t_s, _ = x_sample.shape
    depth = w_ada.shape[0]
    d_a = w_out_a.shape[1]
    d_rnn = w_out_b.shape[1]
    group = d_a // N_GROUPS_A
    block = d_rnn // N_BLOCKS_B
    mp, ms = nb * t_p, ns * t_s
    assert ns == PAT and PAT % nb == 0 and t_p % CHUNK == 0 and t_s <= CHUNK and nb == 4

    x = _to_time_major(x_prompt, x_sample.transpose(1, 0, 2).reshape(ms, d))
    c_pat = jnp.concatenate([jnp.tile(c_prompt, (PAT // nb, 1)), c_sample])
    mod = _ada(c_pat, w_ada, b_ada)


    blocks_per_group = LANES // math.gcd(block, LANES)
    win = (-(-block // LANES) + 1) * LANES
    halo_p = -(-(CONV_W - 1) * nb // SUBLANES) * SUBLANES

    v_new, conv_p, h_p, conv_s, h_s = [], [], [], [], []
    for i in range(depth):
        j = i // 2
        h = _rms_mod(x, g_norm1[i], mod, i, 0, 1, mp)
        if i % 2 == 0:
            u = _mm_bias_act(h, w_in_a, b_in_a, j, 0, d_a, "gelu", BF16)
            v = _mm_bias_act(h, w_in_a, b_in_a, j, d_a, d_a, "gelu", F32)
            vn_s = _layernorm(v, g_v_a[j], b_v_a[j], F32, mp, ms)
            bias_t = jnp.repeat(b_s_a[j], nb, axis=1).T
            y = _gate_prompt(u, v, g_v_a[j], b_v_a[j], w_s_a[j], bias_t, mp, group, nb)
            w_small = jnp.where(jnp.tril(jnp.ones((t_s, t_s), bool)), w_s_a[j][:, :t_s, :t_s], 0.0)
            y = _gate_sample(w_small.reshape(N_GROUPS_A, t_s * t_s), b_s_a[j][:, :t_s], u, vn_s, y, mp, t_s, ns, group)
            v_new.append(vn_s.reshape(t_s, ns, d_a).transpose(1, 0, 2))
            x = _mm_resid(y, w_out_a, j, x, mod, i, 2, mp, 1024, 512, order="nm")
        else:
            gg, w_out_bf = _mm_bias_act(h, w_in_b, b_in_b, j, 0, d_rnn, "gelu", BF16, side=w_out_b)
            xb = _mm_bias_act(h, w_in_b, b_in_b, j, d_rnn, d_rnn, None, F32)
            wra = _pad_block_weights(0.5 * w_ra_b[j], block, blocks_per_group, win)
            wix = _pad_block_weights(0.5 * w_ix_b[j], block, blocks_per_group, win)
            common = (w_conv_b[j], b_conv_b[j], wra, b_ra_b[j], wix, b_ix_b[j], lam_b[j])
            geom = dict(block=block, blocks_per_group=blocks_per_group, win=win)
            y, cp, hp = _rglru(xb, gg, jnp.zeros((halo_p, d_rnn), F32), jnp.zeros((SUBLANES, d_rnn), F32), *common,
                               None, row0=0, nrows=mp, p=nb, bt=512, **geom)
            hist_s = state_conv[j].transpose(1, 0, 2).reshape((CONV_W - 1) * ns, d_rnn)
            y, cs, hs = _rglru(xb, gg, hist_s, state_h[j], *common, y, row0=mp, nrows=ms, p=ns, bt=2 * ns, **geom)
            conv_p.append(cp[halo_p - (CONV_W - 1) * nb:].reshape(CONV_W - 1, nb, d_rnn).transpose(1, 0, 2))
            h_p.append(hp[SUBLANES - nb:])
            conv_s.append(cs.reshape(CONV_W - 1, ns, d_rnn).transpose(1, 0, 2))
            h_s.append(hs)
            x = _mm_resid(y, w_out_bf[None], 0, x, mod, i, 2, mp, 1024, 512, order="nm")
        h = _rms_mod(x, g_norm2[i], mod, i, 3, 4, mp)
        f, w2b = _mm_swiglu(h, w_ff1, w_ff3, w_ff2, i)
        x = _mm_resid(f, w2b[None], 0, x, mod, i, 5, mp, 512, 512, order="nm")

    y_prompt = _rms_seq_major(x, g_final, nb, t_p)
    y_sample = _rms(x, g_final, mp, ms).reshape(t_s, ns, d).transpose(1, 0, 2)
    return (y_prompt, y_sample, jnp.stack(v_new), jnp.stack(conv_p), jnp.stack(h_p), jnp.stack(conv_s),
            jnp.stack(h_s))
```
